```python
import jax, jax.numpy as jnp
from jax import lax
import numpy as np

D_MODEL = 1024
BATCH = 8
SEQ = 4096
DEPTH = 2

D_CONV = D_MODEL
D_SGU = D_MODEL
D_CFM = D_MODEL
N_SGU_GROUPS = 8
SGU_GROUP_DIM = D_SGU // N_SGU_GROUPS
CHUNK = 128
SHORT_K = 3
CFM_K = 31
N_BRANCH = 3
D_FF = -(-8 * D_MODEL // (3 * 256)) * 256
N_MOD = 6
EPS = 1e-6
SPLITS = [D_CONV, D_CONV, D_CONV, D_SGU, D_SGU, D_CFM, D_CFM, D_MODEL, D_MODEL, D_MODEL]
N_IN = sum(SPLITS)
SPLIT_IDX = list(np.cumsum(SPLITS)[:-1])

kernel_name = "hybrid_gated_conv_sgu_conformer_block"


def rms_norm(x, g):
    x32 = x.astype(jnp.float32)
    y = x32 * lax.rsqrt(jnp.mean(jnp.square(x32), axis=-1, keepdims=True) + EPS)
    return y.astype(x.dtype) * g


def layer_norm(x, g, b):
    x32 = x.astype(jnp.float32)
    mu = jnp.mean(x32, axis=-1, keepdims=True)
    var = jnp.mean(jnp.square(x32 - mu), axis=-1, keepdims=True)
    return ((x32 - mu) * lax.rsqrt(var + EPS)).astype(x.dtype) * g + b


def modulate(h, shift, scale):
    return h * (1 + scale[:, None, :]) + shift[:, None, :]


def causal_depthwise_conv(x, w):
    k, ch = w.shape
    return lax.conv_general_dilated(
        x, w[:, None, :].astype(x.dtype), window_strides=(1,), padding=[(k - 1, 0)],
        dimension_numbers=('NWC', 'WIO', 'NWC'), feature_group_count=ch)


def spatial_gating(u, v, ln_g, ln_b, w_s, b_s):
    bn, s, _ = v.shape
    v = layer_norm(v, ln_g, ln_b)
    v = v.reshape(bn, s // CHUNK, CHUNK, N_SGU_GROUPS, SGU_GROUP_DIM)
    mask = jnp.tril(jnp.ones((CHUNK, CHUNK), dtype=bool))
    w = jnp.where(mask[None], w_s, 0).astype(v.dtype)
    mixed = jnp.einsum('gts,bnsgc->bntgc', w, v) + b_s.T[None, None, :, :, None]
    return u * mixed.reshape(bn, s, D_SGU)


def setup_inputs(seed: int = 0) -> dict:
    key = jax.random.key(seed)
    ks = jax.random.split(key, 24)
    f32 = jnp.float32
    L, D = DEPTH, D_MODEL

    def nrm(k, shape, fan_in):
        return jax.random.normal(k, shape, f32) * (fan_in ** -0.5)

    def gain(k, shape):
        return 1.0 + 0.05 * jax.random.normal(k, shape, f32)

    def bias(k, shape):
        return 0.02 * jax.random.normal(k, shape, f32)

    return {
        "x": jax.random.normal(ks[0], (BATCH, SEQ, D), f32),
        "c": jax.random.normal(ks[1], (BATCH, D), f32),
        "w_ada": nrm(ks[2], (L, D, N_MOD * D), D),
        "b_ada": bias(ks[3], (L, N_MOD * D)),
        "norm1_g": gain(ks[4], (L, D)),
        "w_in": nrm(ks[5], (L, D, N_IN), D),
        "w_short": nrm(ks[6], (L, SHORT_K, D_CONV), SHORT_K),
        "w_a_out": nrm(ks[7], (L, D_CONV, D), D_CONV),
        "sgu_ln_g": gain(ks[8], (L, D_SGU)),
        "sgu_ln_b": bias(ks[9], (L, D_SGU)),
        "w_sgu": nrm(ks[10], (L, N_SGU_GROUPS, CHUNK, CHUNK), CHUNK),
        "b_sgu": 1.0 + 0.1 * jax.random.normal(ks[11], (L, N_SGU_GROUPS, CHUNK), f32),
        "w_b_out": nrm(ks[12], (L, D_SGU, D), D_SGU),
        "cfm_conv_w": nrm(ks[13], (L, CFM_K, D_CFM), CFM_K),
        "cfm_conv_b": bias(ks[14], (L, D_CFM)),
        "cfm_ln_g": gain(ks[15], (L, D_CFM)),
        "cfm_ln_b": bias(ks[16], (L, D_CFM)),
        "w_c_out": nrm(ks[17], (L, D_CFM, D), D_CFM),
        "w_o": nrm(ks[18], (L, D, D), D),
        "norm2_g": gain(ks[19], (L, D)),
        "w_ffn_in": nrm(ks[20], (L, D, 2 * D_FF), D),
        "w_ffn_out": nrm(ks[21], (L, D_FF, D), D_FF),
        "final_g": gain(ks[22], (D,)),
    }


def reference(x, c, w_ada, b_ada, norm1_g, w_in, w_short, w_a_out, sgu_ln_g, sgu_ln_b,
              w_sgu, b_sgu, w_b_out, cfm_conv_w, cfm_conv_b, cfm_ln_g, cfm_ln_b, w_c_out,
              w_o, norm2_g, w_ffn_in, w_ffn_out, final_g):
    bn, s, d = x.shape
    c_act = jax.nn.silu(c)
    for l in range(DEPTH):
        mod = c_act @ w_ada[l] + b_ada[l]
        shift1, scale1, gate1, shift2, scale2, gate2 = jnp.split(mod, N_MOD, axis=-1)

        h = modulate(rms_norm(x, norm1_g[l]), shift1, scale1)
        z = h @ w_in[l]
        b_a, c_a, x_a, u, v, a_c, g_c, gate_a, gate_b, gate_c = jnp.split(z, SPLIT_IDX, axis=-1)

        y_a = (b_a * causal_depthwise_conv(c_a * x_a, w_short[l])) @ w_a_out[l]

        y_b = spatial_gating(jax.nn.gelu(u), jax.nn.gelu(v), sgu_ln_g[l], sgu_ln_b[l],
                             w_sgu[l], b_sgu[l]) @ w_b_out[l]

        glu = a_c * jax.nn.sigmoid(g_c)
        conv = causal_depthwise_conv(glu, cfm_conv_w[l]) + cfm_conv_b[l]
        y_c = jax.nn.silu(layer_norm(conv, cfm_ln_g[l], cfm_ln_b[l])) @ w_c_out[l]

        merged = (jax.nn.sigmoid(gate_a) * y_a + jax.nn.sigmoid(gate_b) * y_b
                  + jax.nn.sigmoid(gate_c) * y_c)
        x = x + gate1[:, None, :] * (merged @ w_o[l])

        h2 = modulate(rms_norm(x, norm2_g[l]), shift2, scale2)
        g_f, u_f = jnp.split(h2 @ w_ffn_in[l], 2, axis=-1)
        x = x + gate2[:, None, :] * ((jax.nn.silu(g_f) * u_f) @ w_ffn_out[l])

    return rms_norm(x, final_g)
```

```python
import functools

import jax
import jax.numpy as jnp
from jax import lax
from jax.experimental import pallas as pl
from jax.experimental.pallas import tpu as pltpu

EPS = 1e-6
CHUNK = 128
N_SGU_GROUPS = 8
SHORT_K = 3
CFM_K = 31
N_MOD = 6

TOK_TILE = 512
COL_BLOCK = 512
CONV_ROWS = 64
SUBLANES = 8
LANES = 128
SHORT_HALO = 8
CFM_HALO = 32
VMEM_LIMIT_BYTES = 60 * 1024 * 1024

F32 = jnp.float32
BF16 = jnp.bfloat16


def _dot(a, b):
    return jnp.dot(a, b, preferred_element_type=F32)


def _sigmoid(x):
    return 1.0 / (1.0 + jnp.exp(-x))


def _silu(x):
    return x * _sigmoid(x)


def _gelu_tanh(x):
    c = 0.7978845608028654
    return 0.5 * x * (1.0 + jnp.tanh(c * (x + 0.044715 * (x * x * x))))


def _row_stats(v, n):
    mu = jnp.sum(v, axis=-1, keepdims=True) * (1.0 / n)
    ex2 = jnp.sum(v * v, axis=-1, keepdims=True) * (1.0 / n)
    return mu, lax.rsqrt(ex2 - mu * mu + EPS)


def _modulated_rms(x, g, shift, scale):
    ms = jnp.mean(x * x, axis=-1, keepdims=True)
    return (x * lax.rsqrt(ms + EPS) * g) * (1.0 + scale) + shift


def _causal_conv_rows(win, w_ref, ls):
    k_taps = w_ref.shape[0]
    rows = win.shape[0] - CFM_HALO
    out = None
    for res in range(SUBLANES):
        lead = 0 if res == 0 else SUBLANES
        part = None
        for q in range((k_taps - 1 - res) // SUBLANES + 1):
            lag = SUBLANES * q + res
            lo = CFM_HALO - lead - SUBLANES * q
            term = w_ref[k_taps - 1 - lag:k_taps - lag, ls] * win[lo:lo + lead + rows, :]
            part = term if part is None else part + term
        if res == 0:
            out = part
        else:
            out = out + pltpu.roll(part, res, axis=0)[lead:, :]
    return out


def _mod_kernel(c_ref, w_ref, b_ref, o_ref):
    c = c_ref[...]
    o_ref[0] = jnp.dot(_silu(c), w_ref[0], preferred_element_type=F32,
                       precision=lax.Precision.HIGHEST) + b_ref[0]


def _modulation(c, w_ada, b_ada):
    depth, d, n = w_ada.shape
    bn = c.shape[0]
    nb = n // 4
    out = pl.pallas_call(
        _mod_kernel,
        grid=(depth, n // nb),
        in_specs=[
            pl.BlockSpec((bn, d), lambda l, j: (0, 0)),
            pl.BlockSpec((1, d, nb), lambda l, j: (l, 0, j)),
            pl.BlockSpec((1, 1, nb), lambda l, j: (l, 0, j)),
        ],
        out_specs=pl.BlockSpec((1, bn, nb), lambda l, j: (l, 0, j)),
        out_shape=jax.ShapeDtypeStruct((depth, bn, n), F32),
        name="adaln_modulation",
    )(c, w_ada, b_ada.reshape(depth, 1, n))
    return out.reshape(depth, bn, N_MOD, d)


def _mixer_kernel(x_ref, mod_ref, n1g_ref, w_in_ref, wsh_ref, w_a_ref, lng_ref, lnb_ref,
                  wsgu_ref, bsgu_ref, w_b_ref, cw_ref, cb_ref, clg_ref, clb_ref, w_c_ref,
                  w_o_ref, out_ref,
                  h_buf, cx_buf, glu_buf, ya_buf, yb_buf, yc_buf, gu_buf, ln_buf, mg_buf):
    t, d = h_buf.shape
    s = pl.program_id(1)

    @pl.when(s == 0)
    def _():
        cx_buf[0:SHORT_HALO, :] = jnp.zeros((SHORT_HALO, d), F32)
        glu_buf[0:CFM_HALO, :] = jnp.zeros((CFM_HALO, d), F32)

    h_buf[...] = _modulated_rms(x_ref[0], n1g_ref[...], mod_ref[0, 0:1, :],
                                mod_ref[0, 1:2, :]).astype(BF16)

    col_blocks = [slice(j, j + COL_BLOCK) for j in range(0, d, COL_BLOCK)]

    def zcols(i, cs):
        return slice(i * d + cs.start, i * d + cs.stop)

    for cs in col_blocks:
        hb = h_buf[...]
        b_a = _dot(hb, w_in_ref[:, zcols(0, cs)])
        c_a = _dot(hb, w_in_ref[:, zcols(1, cs)])
        x_a = _dot(hb, w_in_ref[:, zcols(2, cs)])
        cx = c_a * x_a
        cx_buf[SHORT_HALO:SHORT_HALO + t, cs] = cx
        win = cx_buf[0:SHORT_HALO + t, cs]
        conv = wsh_ref[SHORT_K - 1:SHORT_K, cs] * cx
        for lag in range(1, SHORT_K):
            k = SHORT_K - 1 - lag
            conv = conv + wsh_ref[k:k + 1, cs] * pltpu.roll(win, lag, axis=0)[SHORT_HALO:, :]
        ya_buf[:, cs] = (b_a * conv).astype(BF16)

    for cs in col_blocks:
        hb = h_buf[...]
        u = _dot(hb, w_in_ref[:, zcols(3, cs)])
        v = _dot(hb, w_in_ref[:, zcols(4, cs)])
        gu_buf[:, cs] = _gelu_tanh(u).astype(BF16)
        ln_buf[:, cs] = _gelu_tanh(v)
    mu, rstd = _row_stats(ln_buf[...], d)
    tri = (lax.broadcasted_iota(jnp.int32, (CHUNK, CHUNK), 0)
           >= lax.broadcasted_iota(jnp.int32, (CHUNK, CHUNK), 1))
    gd = d // N_SGU_GROUPS
    for g in range(N_SGU_GROUPS):
        gs = slice(g * gd, (g + 1) * gd)
        vn = ((ln_buf[:, gs] - mu) * rstd * lng_ref[:, gs] + lnb_ref[:, gs]).astype(BF16)
        wg = jnp.where(tri, wsgu_ref[g], 0.0).astype(BF16)
        bg = bsgu_ref[:, g:g + 1]
        for n in range(t // CHUNK):
            rs = slice(n * CHUNK, (n + 1) * CHUNK)
            mixed = _dot(wg, vn[rs, :]) + bg
            yb_buf[rs, gs] = (gu_buf[rs, gs].astype(F32) * mixed).astype(BF16)

    for cs in col_blocks:
        hb = h_buf[...]
        a_c = _dot(hb, w_in_ref[:, zcols(5, cs)])
        g_c = _dot(hb, w_in_ref[:, zcols(6, cs)])
        glu_buf[CFM_HALO:CFM_HALO + t, cs] = a_c * _sigmoid(g_c)

        def conv_rows(i, carry, cs=cs):
            r0 = pl.multiple_of(i * CONV_ROWS, CONV_ROWS)
            for l0 in range(cs.start, cs.stop, LANES):
                ls = slice(l0, l0 + LANES)
                ln_buf[pl.ds(r0, CONV_ROWS), ls] = _causal_conv_rows(
                    glu_buf[pl.ds(r0, CFM_HALO + CONV_ROWS), ls], cw_ref, ls) + cb_ref[:, ls]
            return carry

        lax.fori_loop(0, t // CONV_ROWS, conv_rows, 0)
    mu, rstd = _row_stats(ln_buf[...], d)
    yc_buf[...] = _silu((ln_buf[...] - mu) * rstd * clg_ref[...] + clb_ref[...]).astype(BF16)

    for cs in col_blocks:
        hb = h_buf[...]
        m = _sigmoid(_dot(hb, w_in_ref[:, zcols(7, cs)])) * _dot(ya_buf[...], w_a_ref[:, cs])
        m = m + _sigmoid(_dot(hb, w_in_ref[:, zcols(8, cs)])) * _dot(yb_buf[...], w_b_ref[:, cs])
        m = m + _sigmoid(_dot(hb, w_in_ref[:, zcols(9, cs)])) * _dot(yc_buf[...], w_c_ref[:, cs])
        mg_buf[:, cs] = m.astype(BF16)
    for cs in col_blocks:
        o = _dot(mg_buf[...], w_o_ref[:, cs])
        out_ref[0, :, cs] = x_ref[0, :, cs] + mod_ref[0, 2:3, cs] * o

    cx_buf[0:SHORT_HALO, :] = cx_buf[t:t + SHORT_HALO, :]
    glu_buf[0:CFM_HALO, :] = glu_buf[t:t + CFM_HALO, :]


def _resident(shape):
    zeros = (0,) * len(shape)
    return pl.BlockSpec(shape, lambda b, s: zeros, pipeline_mode=pl.Buffered(1))


def _mixer(x, mod, n1g, w_in, wsh, w_a, lng, lnb, wsgu, bsgu_t, w_b, cw, cb, clg, clb, w_c, w_o):
    bn, sq, d = x.shape
    t = TOK_TILE
    params = (n1g, w_in, wsh, w_a, lng, lnb, wsgu, bsgu_t, w_b, cw, cb, clg, clb, w_c, w_o)
    return pl.pallas_call(
        _mixer_kernel,
        grid=(bn, sq // t),
        in_specs=[pl.BlockSpec((1, t, d), lambda b, s: (b, s, 0)),
                  pl.BlockSpec((1, N_MOD, d), lambda b, s: (b, 0, 0))]
                 + [_resident(p.shape) for p in params],
        out_specs=pl.BlockSpec((1, t, d), lambda b, s: (b, s, 0)),
        out_shape=jax.ShapeDtypeStruct(x.shape, F32),
        scratch_shapes=[
            pltpu.VMEM((t, d), BF16),
            pltpu.VMEM((SHORT_HALO + t, d), F32),
            pltpu.VMEM((CFM_HALO + t, d), F32),
            pltpu.VMEM((t, d), BF16),
            pltpu.VMEM((t, d), BF16),
            pltpu.VMEM((t, d), BF16),
            pltpu.VMEM((t, d), BF16),
            pltpu.VMEM((t, d), F32),
            pltpu.VMEM((t, d), BF16),
        ],
        compiler_params=pltpu.CompilerParams(
            dimension_semantics=("arbitrary", "arbitrary"),
            vmem_limit_bytes=VMEM_LIMIT_BYTES),
        name="token_mixer",
    )(x, mod, *params)


def _ffn_kernel(x_ref, mod_ref, n2g_ref, w_in_ref, w_out_ref, fg_ref, out_ref, h_buf, a_buf,
                *, final_norm):
    t, d = h_buf.shape
    d_ff = a_buf.shape[1]
    h_buf[...] = _modulated_rms(x_ref[0], n2g_ref[...], mod_ref[0, 3:4, :],
                                mod_ref[0, 4:5, :]).astype(BF16)
    for j in range(0, d_ff, COL_BLOCK):
        cs = slice(j, min(j + COL_BLOCK, d_ff))
        us = slice(d_ff + cs.start, d_ff + cs.stop)
        hb = h_buf[...]
        a_buf[:, cs] = (_silu(_dot(hb, w_in_ref[:, cs])) * _dot(hb, w_in_ref[:, us])).astype(BF16)
    for j in range(0, d, COL_BLOCK):
        cs = slice(j, j + COL_BLOCK)
        o = _dot(a_buf[...], w_out_ref[:, cs])
        out_ref[0, :, cs] = x_ref[0, :, cs] + mod_ref[0, 5:6, cs] * o
    if final_norm:
        y = out_ref[0]
        ms = jnp.mean(y * y, axis=-1, keepdims=True)
        out_ref[0] = y * lax.rsqrt(ms + EPS) * fg_ref[...]


def _ffn(x, mod, n2g, w_in, w_out, fg, *, final_norm):
    bn, sq, d = x.shape
    d_ff = w_out.shape[0]
    t = TOK_TILE
    params = (n2g, w_in, w_out, fg)
    return pl.pallas_call(
        functools.partial(_ffn_kernel, final_norm=final_norm),
        grid=(bn, sq // t),
        in_specs=[pl.BlockSpec((1, t, d), lambda b, s: (b, s, 0)),
                  pl.BlockSpec((1, N_MOD, d), lambda b, s: (b, 0, 0))]
                 + [_resident(p.shape) for p in params],
        out_specs=pl.BlockSpec((1, t, d), lambda b, s: (b, s, 0)),
        out_shape=jax.ShapeDtypeStruct(x.shape, F32),
        scratch_shapes=[pltpu.VMEM((t, d), BF16), pltpu.VMEM((t, d_ff), BF16)],
        compiler_params=pltpu.CompilerParams(
            dimension_semantics=("arbitrary", "arbitrary"),
            vmem_limit_bytes=VMEM_LIMIT_BYTES),
        name="swiglu_final" if final_norm else "swiglu",
    )(x, mod, *params)


def kernel(x, c, w_ada, b_ada, norm1_g, w_in, w_short, w_a_out, sgu_ln_g, sgu_ln_b, w_sgu, b_sgu,
           w_b_out, cfm_conv_w, cfm_conv_b, cfm_ln_g, cfm_ln_b, w_c_out, w_o, norm2_g, w_ffn_in,
           w_ffn_out, final_g):
    depth = w_in.shape[0]
    d = x.shape[-1]
    assert x.shape[1] % TOK_TILE == 0 and TOK_TILE % CHUNK == 0 and d % COL_BLOCK == 0
    assert TOK_TILE % CONV_ROWS == 0 and w_sgu.shape[-1] == CHUNK
    assert w_short.shape[1] == SHORT_K and cfm_conv_w.shape[1] == CFM_K

    mod = _modulation(c, w_ada, b_ada)
    row = lambda p: p.reshape(1, -1)
    fg = row(final_g)
    for l in range(depth):
        x = _mixer(x, mod[l], row(norm1_g[l]), w_in[l].astype(BF16), w_short[l],
                   w_a_out[l].astype(BF16), row(sgu_ln_g[l]), row(sgu_ln_b[l]), w_sgu[l],
                   b_sgu[l].T, w_b_out[l].astype(BF16), cfm_conv_w[l], row(cfm_conv_b[l]),
                   row(cfm_ln_g[l]), row(cfm_ln_b[l]), w_c_out[l].astype(BF16),
                   w_o[l].astype(BF16))
        x = _ffn(x, mod[l], row(norm2_g[l]), w_ffn_in[l].astype(BF16),
                 w_ffn_out[l].astype(BF16), fg, final_norm=(l == depth - 1))
    return x
```

```python
import functools

import jax
import jax.numpy as jnp
from jax import lax
from jax.experimental import pallas as pl
from jax.experimental.pallas import tpu as pltpu

EPS = 1e-6
CHUNK = 128
N_SGU_GROUPS = 8
SHORT_K = 3
CFM_K = 31
N_MOD = 6

TOK_TILE = 512
COL_BLOCK = 512
CONV_ROWS = 64
SUBLANES = 8
LANES = 128
SHORT_HALO = 8
CFM_HALO = 32
VMEM_LIMIT_BYTES = 60 * 1024 * 1024

F32 = jnp.float32
BF16 = jnp.bfloat16


def _dot(a, b):
    return jnp.dot(a, b, preferred_element_type=F32)


def _sigmoid(x):
    return 1.0 / (1.0 + jnp.exp(-x))


def _silu(x):
    return x * _sigmoid(x)


def _gelu_tanh(x):
    c = 0.7978845608028654
    return 0.5 * x * (1.0 + jnp.tanh(c * (x + 0.044715 * (x * x * x))))


def _row_stats(v, n):
    mu = jnp.sum(v, axis=-1, keepdims=True) * (1.0 / n)
    ex2 = jnp.sum(v * v, axis=-1, keepdims=True) * (1.0 / n)
    return mu, lax.rsqrt(ex2 - mu * mu + EPS)


def _modulated_rms(x, g, shift, scale):
    ms = jnp.mean(x * x, axis=-1, keepdims=True)
    return (x * lax.rsqrt(ms + EPS) * g) * (1.0 + scale) + shift


def _causal_conv_rows(win, w_ref, ls):
    k_taps = w_ref.shape[0]
    rows = win.shape[0] - CFM_HALO
    out = None
    for res in range(SUBLANES):
        lead = 0 if res == 0 else SUBLANES
        part = None
        for q in range((k_taps - 1 - res) // SUBLANES + 1):
            lag = SUBLANES * q + res
            lo = CFM_HALO - lead - SUBLANES * q
            term = w_ref[k_taps - 1 - lag:k_taps - lag, ls] * win[lo:lo + lead + rows, :]
            part = term if part is None else part + term
        if res == 0:
            out = part
        else:
            out = out + pltpu.roll(part, res, axis=0)[lead:, :]
    return out


def _mod_kernel(c_ref, w_ref, b_ref, o_ref):
    c = c_ref[...]
    o_ref[0] = jnp.dot(_silu(c), w_ref[0], preferred_element_type=F32,
                       precision=lax.Precision.HIGHEST) + b_ref[0]


def _modulation(c, w_ada, b_ada):
    depth, d, n = w_ada.shape
    bn = c.shape[0]
    nb = n // 4
    out = pl.pallas_call(
        _mod_kernel,
        grid=(depth, n // nb),
        in_specs=[
            pl.BlockSpec((bn, d), lambda l, j: (0, 0)),
            pl.BlockSpec((1, d, nb), lambda l, j: (l, 0, j)),
            pl.BlockSpec((1, 1, nb), lambda l, j: (l, 0, j)),
        ],
        out_specs=pl.BlockSpec((1, bn, nb), lambda l, j: (l, 0, j)),
        out_shape=jax.ShapeDtypeStruct((depth, bn, n), F32),
        name="adaln_modulation",
    )(c, w_ada, b_ada.reshape(depth, 1, n))
    return out.reshape(depth, bn, N_MOD, d)


def _mixer_kernel(x_ref, mod_ref, n1g_ref, w_in_ref, wsh_ref, w_a_ref, lng_ref, lnb_ref,
                  wsgu_ref, bsgu_ref, w_b_ref, cw_ref, cb_ref, clg_ref, clb_ref, w_c_ref,
                  w_o_ref, out_ref,
                  h_buf, cx_buf, glu_buf, ya_buf, yb_buf, yc_buf, gu_buf, ln_buf, ln_c_buf,
                  mg_buf):
    t, d = h_buf.shape
    s = pl.program_id(1)

    @pl.when(s == 0)
    def _():
        cx_buf[0:SHORT_HALO, :] = jnp.zeros((SHORT_HALO, d), F32)
        glu_buf[0:CFM_HALO, :] = jnp.zeros((CFM_HALO, d), F32)

    h_buf[...] = _modulated_rms(x_ref[0], n1g_ref[...], mod_ref[0, 0:1, :],
                                mod_ref[0, 1:2, :]).astype(BF16)

    col_blocks = [slice(j, j + COL_BLOCK) for j in range(0, d, COL_BLOCK)]

    def zcols(i, cs):
        return slice(i * d + cs.start, i * d + cs.stop)

    for cs in col_blocks:
        hb = h_buf[...]
        a_c = _dot(hb, w_in_ref[:, zcols(5, cs)])
        g_c = _dot(hb, w_in_ref[:, zcols(6, cs)])
        glu_buf[CFM_HALO:CFM_HALO + t, cs] = a_c * _sigmoid(g_c)

    conv_jobs = [(r0, l0) for l0 in range(0, d, LANES) for r0 in range(0, t, CONV_ROWS)]

    def run_conv_jobs(n):
        for _ in range(min(n, len(conv_jobs))):
            r0, l0 = conv_jobs.pop(0)
            ls = slice(l0, l0 + LANES)
            ln_c_buf[r0:r0 + CONV_ROWS, ls] = _causal_conv_rows(
                glu_buf[r0:r0 + CFM_HALO + CONV_ROWS, ls], cw_ref, ls) + cb_ref[:, ls]

    n_slots = 3 * len(col_blocks)
    jobs_per_slot = -(-len(conv_jobs) // n_slots)

    for cs in col_blocks:
        hb = h_buf[...]
        b_a = _dot(hb, w_in_ref[:, zcols(0, cs)])
        c_a = _dot(hb, w_in_ref[:, zcols(1, cs)])
        x_a = _dot(hb, w_in_ref[:, zcols(2, cs)])
        cx = c_a * x_a
        cx_buf[SHORT_HALO:SHORT_HALO + t, cs] = cx
        win = cx_buf[0:SHORT_HALO + t, cs]
        conv = wsh_ref[SHORT_K - 1:SHORT_K, cs] * cx
        for lag in range(1, SHORT_K):
            k = SHORT_K - 1 - lag
            conv = conv + wsh_ref[k:k + 1, cs] * pltpu.roll(win, lag, axis=0)[SHORT_HALO:, :]
        ya_buf[:, cs] = (b_a * conv).astype(BF16)
        run_conv_jobs(jobs_per_slot)

    for cs in col_blocks:
        hb = h_buf[...]
        u = _dot(hb, w_in_ref[:, zcols(3, cs)])
        v = _dot(hb, w_in_ref[:, zcols(4, cs)])
        gu_buf[:, cs] = _gelu_tanh(u).astype(BF16)
        ln_buf[:, cs] = _gelu_tanh(v)
        run_conv_jobs(jobs_per_slot)
    mu, rstd = _row_stats(ln_buf[...], d)
    tri = (lax.broadcasted_iota(jnp.int32, (CHUNK, CHUNK), 0)
           >= lax.broadcasted_iota(jnp.int32, (CHUNK, CHUNK), 1))
    gd = d // N_SGU_GROUPS
    for g in range(N_SGU_GROUPS):
        gs = slice(g * gd, (g + 1) * gd)
        vn = ((ln_buf[:, gs] - mu) * rstd * lng_ref[:, gs] + lnb_ref[:, gs]).astype(BF16)
        wg = jnp.where(tri, wsgu_ref[g], 0.0).astype(BF16)
        bg = bsgu_ref[:, g:g + 1]
        for n in range(t // CHUNK):
            rs = slice(n * CHUNK, (n + 1) * CHUNK)
            mixed = _dot(wg, vn[rs, :]) + bg
            yb_buf[rs, gs] = (gu_buf[rs, gs].astype(F32) * mixed).astype(BF16)

    for cs in col_blocks:
        hb = h_buf[...]
        m = _sigmoid(_dot(hb, w_in_ref[:, zcols(7, cs)])) * _dot(ya_buf[...], w_a_ref[:, cs])
        m = m + _sigmoid(_dot(hb, w_in_ref[:, zcols(8, cs)])) * _dot(yb_buf[...], w_b_ref[:, cs])
        ln_buf[:, cs] = m
        run_conv_jobs(jobs_per_slot)

    run_conv_jobs(len(conv_jobs))
    mu, rstd = _row_stats(ln_c_buf[...], d)
    yc_buf[...] = _silu((ln_c_buf[...] - mu) * rstd * clg_ref[...] + clb_ref[...]).astype(BF16)

    for cs in col_blocks:
        hb = h_buf[...]
        m = ln_buf[:, cs] + (_sigmoid(_dot(hb, w_in_ref[:, zcols(9, cs)]))
                             * _dot(yc_buf[...], w_c_ref[:, cs]))
        mg_buf[:, cs] = m.astype(BF16)
    for cs in col_blocks:
        o = _dot(mg_buf[...], w_o_ref[:, cs])
        out_ref[0, :, cs] = x_ref[0, :, cs] + mod_ref[0, 2:3, cs] * o

    cx_buf[0:SHORT_HALO, :] = cx_buf[t:t + SHORT_HALO, :]
    glu_buf[0:CFM_HALO, :] = glu_buf[t:t + CFM_HALO, :]


def _resident(shape):
    zeros = (0,) * len(shape)
    return pl.BlockSpec(shape, lambda b, s: zeros, pipeline_mode=pl.Buffered(1))


def _mixer(x, mod, n1g, w_in, wsh, w_a, lng, lnb, wsgu, bsgu_t, w_b, cw, cb, clg, clb, w_c, w_o):
    bn, sq, d = x.shape
    t = TOK_TILE
    params = (n1g, w_in, wsh, w_a, lng, lnb, wsgu, bsgu_t, w_b, cw, cb, clg, clb, w_c, w_o)
    return pl.pallas_call(
        _mixer_kernel,
        grid=(bn, sq // t),
        in_specs=[pl.BlockSpec((1, t, d), lambda b, s: (b, s, 0)),
                  pl.BlockSpec((1, N_MOD, d), lambda b, s: (b, 0, 0))]
                 + [_resident(p.shape) for p in params],
        out_specs=pl.BlockSpec((1, t, d), lambda b, s: (b, s, 0)),
        out_shape=jax.ShapeDtypeStruct(x.shape, F32),
        scratch_shapes=[
            pltpu.VMEM((t, d), BF16),
            pltpu.VMEM((SHORT_HALO + t, d), F32),
            pltpu.VMEM((CFM_HALO + t, d), F32),
            pltpu.VMEM((t, d), BF16),
            pltpu.VMEM((t, d), BF16),
            pltpu.VMEM((t, d), BF16),
            pltpu.VMEM((t, d), BF16),
            pltpu.VMEM((t, d), F32),
            pltpu.VMEM((t, d), F32),
            pltpu.VMEM((t, d), BF16),
        ],
        compiler_params=pltpu.CompilerParams(
            dimension_semantics=("arbitrary", "arbitrary"),
            vmem_limit_bytes=VMEM_LIMIT_BYTES),
        name="token_mixer",
    )(x, mod, *params)


def _ffn_kernel(x_ref, mod_ref, n2g_ref, w_in_ref, w_out_ref, fg_ref, out_ref, h_buf, a_buf,
                *, final_norm):
    t, d = h_buf.shape
    d_ff = a_buf.shape[1]
    h_buf[...] = _modulated_rms(x_ref[0], n2g_ref[...], mod_ref[0, 3:4, :],
                                mod_ref[0, 4:5, :]).astype(BF16)
    for j in range(0, d_ff, COL_BLOCK):
        cs = slice(j, min(j + COL_BLOCK, d_ff))
        us = slice(d_ff + cs.start, d_ff + cs.stop)
        hb = h_buf[...]
        a_buf[:, cs] = (_silu(_dot(hb, w_in_ref[:, cs])) * _dot(hb, w_in_ref[:, us])).astype(BF16)
    for j in range(0, d, COL_BLOCK):
        cs = slice(j, j + COL_BLOCK)
        o = _dot(a_buf[...], w_out_ref[:, cs])
        out_ref[0, :, cs] = x_ref[0, :, cs] + mod_ref[0, 5:6, cs] * o
    if final_norm:
        y = out_ref[0]
        ms = jnp.mean(y * y, axis=-1, keepdims=True)
        out_ref[0] = y * lax.rsqrt(ms + EPS) * fg_ref[...]


def _ffn(x, mod, n2g, w_in, w_out, fg, *, final_norm):
    bn, sq, d = x.shape
    d_ff = w_out.shape[0]
    t = TOK_TILE
    params = (n2g, w_in, w_out, fg)
    return pl.pallas_call(
        functools.partial(_ffn_kernel, final_norm=final_norm),
        grid=(bn, sq // t),
        in_specs=[pl.BlockSpec((1, t, d), lambda b, s: (b, s, 0)),
                  pl.BlockSpec((1, N_MOD, d), lambda b, s: (b, 0, 0))]
                 + [_resident(p.shape) for p in params],
        out_specs=pl.BlockSpec((1, t, d), lambda b, s: (b, s, 0)),
        out_shape=jax.ShapeDtypeStruct(x.shape, F32),
        scratch_shapes=[pltpu.VMEM((t, d), BF16), pltpu.VMEM((t, d_ff), BF16)],
        compiler_params=pltpu.CompilerParams(
            dimension_semantics=("arbitrary", "arbitrary"),
            vmem_limit_bytes=VMEM_LIMIT_BYTES),
        name="swiglu_final" if final_norm else "swiglu",
    )(x, mod, *params)


def kernel(x, c, w_ada, b_ada, norm1_g, w_in, w_short, w_a_out, sgu_ln_g, sgu_ln_b, w_sgu, b_sgu,
           w_b_out, cfm_conv_w, cfm_conv_b, cfm_ln_g, cfm_ln_b, w_c_out, w_o, norm2_g, w_ffn_in,
           w_ffn_out, final_g):
    depth = w_in.shape[0]
    d = x.shape[-1]
    assert x.shape[1] % TOK_TILE == 0 and TOK_TILE % CHUNK == 0 and d % COL_BLOCK == 0
    assert TOK_TILE % CONV_ROWS == 0 and w_sgu.shape[-1] == CHUNK
    assert w_short.shape[1] == SHORT_K and cfm_conv_w.shape[1] == CFM_K

    mod = _modulation(c, w_ada, b_ada)
    row = lambda p: p.reshape(1, -1)
    fg = row(final_g)
    for l in range(depth):
        x = _mixer(x, mod[l], row(norm1_g[l]), w_in[l].astype(BF16), w_short[l],
                   w_a_out[l].astype(BF16), row(sgu_ln_g[l]), row(sgu_ln_b[l]), w_sgu[l],
                   b_sgu[l].T, w_b_out[l].astype(BF16), cfm_conv_w[l], row(cfm_conv_b[l]),
                   row(cfm_ln_g[l]), row(cfm_ln_b[l]), w_c_out[l].astype(BF16),
                   w_o[l].astype(BF16))
        x = _ffn(x, mod[l], row(norm2_g[l]), w_ffn_in[l].astype(BF16),
                 w_ffn_out[l].astype(BF16), fg, final_norm=(l == depth - 1))
    return x
```

```python
import functools

import jax
import jax.numpy as jnp
from jax import lax
from jax.experimental import pallas as pl
from jax.experimental.pallas import tpu as pltpu

EPS = 1e-6
CHUNK = 128
N_SGU_GROUPS = 8
SHORT_K = 3
CFM_K = 31
N_MOD = 6
N_BRANCH = 3

TOK_TILE = 512
COL_BLOCK = 512
CONV_ROWS = 64
CONV_JOBS_PER_DOT = 4
SUBLANES = 8
LANES = 128
SHORT_HALO = 8
CFM_HALO = 32
VMEM_LIMIT_BYTES = 62 * 1024 * 1024

F32 = jnp.float32
BF16 = jnp.bfloat16


def _dot(a, b):
    return jnp.dot(a, b, preferred_element_type=F32)


def _sigmoid(x):
    return 1.0 / (1.0 + jnp.exp(-x))


def _silu(x):
    return x * _sigmoid(x)


def _gelu_tanh(x):
    c = 0.7978845608028654
    return 0.5 * x * (1.0 + jnp.tanh(c * (x + 0.044715 * (x * x * x))))


def _row_stats(v, n):
    mu = jnp.sum(v, axis=-1, keepdims=True) * (1.0 / n)
    ex2 = jnp.sum(v * v, axis=-1, keepdims=True) * (1.0 / n)
    return mu, lax.rsqrt(ex2 - mu * mu + EPS)


def _modulated_rms(x, g, shift, scale):
    ms = jnp.mean(x * x, axis=-1, keepdims=True)
    return (x * lax.rsqrt(ms + EPS) * g) * (1.0 + scale) + shift


def _causal_conv_rows(win, w_ref, ls):
    k_taps = w_ref.shape[0]
    rows = win.shape[0] - CFM_HALO
    out = None
    for res in range(SUBLANES):
        lead = 0 if res == 0 else SUBLANES
        part = None
        for q in range((k_taps - 1 - res) // SUBLANES + 1):
            lag = SUBLANES * q + res
            lo = CFM_HALO - lead - SUBLANES * q
            term = w_ref[k_taps - 1 - lag:k_taps - lag, ls] * win[lo:lo + lead + rows, :]
            part = term if part is None else part + term
        if res == 0:
            out = part
        else:
            out = out + pltpu.roll(part, res, axis=0)[lead:, :]
    return out


def _mod_kernel(c_ref, w_ref, b_ref, o_ref):
    c = c_ref[...]
    o_ref[0] = jnp.dot(_silu(c), w_ref[0], preferred_element_type=F32,
                       precision=lax.Precision.HIGHEST) + b_ref[0]


def _modulation(c, w_ada, b_ada):
    depth, d, n = w_ada.shape
    bn = c.shape[0]
    nb = n // 4
    out = pl.pallas_call(
        _mod_kernel,
        grid=(depth, n // nb),
        in_specs=[
            pl.BlockSpec((bn, d), lambda l, j: (0, 0)),
            pl.BlockSpec((1, d, nb), lambda l, j: (l, 0, j)),
            pl.BlockSpec((1, 1, nb), lambda l, j: (l, 0, j)),
        ],
        out_specs=pl.BlockSpec((1, bn, nb), lambda l, j: (l, 0, j)),
        out_shape=jax.ShapeDtypeStruct((depth, bn, n), F32),
        name="adaln_modulation",
    )(c, w_ada, b_ada.reshape(depth, 1, n))
    return out.reshape(depth, bn, N_MOD, d)


def _mixer_kernel(x_ref, mod_ref, n1g_ref, w_in_ref, wsh_ref, w_a_ref, lng_ref, lnb_ref,
                  wsgu_ref, bsgu_ref, w_b_ref, cw_ref, cb_ref, clg_ref, clb_ref, w_c_ref,
                  w_o_ref, out_ref,
                  h_buf, cx_buf, cx_halo, glu_buf, ya_buf, yb_buf, yc_buf, gu_buf, ln_buf,
                  ln_c_buf, gate_buf):
    t, d = h_buf.shape
    s = pl.program_id(1)

    @pl.when(s == 0)
    def _():
        cx_halo[...] = jnp.zeros((SHORT_HALO, d), F32)
        glu_buf[0:CFM_HALO, :] = jnp.zeros((CFM_HALO, d), F32)

    col_blocks = [slice(j, j + COL_BLOCK) for j in range(0, d, COL_BLOCK)]

    conv_jobs = [(r0, l0) for l0 in range(0, d, LANES) for r0 in range(0, t, CONV_ROWS)]

    def run_conv_jobs(n):
        for _ in range(min(n, len(conv_jobs))):
            r0, l0 = conv_jobs.pop(0)
            ls = slice(l0, l0 + LANES)
            ln_c_buf[r0:r0 + CONV_ROWS, ls] = _causal_conv_rows(
                glu_buf[r0:r0 + CFM_HALO + CONV_ROWS, ls], cw_ref, ls) + cb_ref[:, ls]

    def h_dot(i, cs, jobs=CONV_JOBS_PER_DOT):
        y = _dot(h_buf[...], w_in_ref[:, i * d + cs.start:i * d + cs.stop])
        run_conv_jobs(jobs)
        return y

    h_buf[...] = _modulated_rms(x_ref[0], n1g_ref[...], mod_ref[0, 0:1, :],
                                mod_ref[0, 1:2, :]).astype(BF16)

    for cs in col_blocks:
        first = cs.start == 0
        a_c = h_dot(5, cs, jobs=0 if first else CONV_JOBS_PER_DOT)
        g_c = h_dot(6, cs, jobs=0 if first else CONV_JOBS_PER_DOT)
        glu_buf[CFM_HALO:CFM_HALO + t, cs] = a_c * _sigmoid(g_c)

    for cs in col_blocks:
        b_a = h_dot(0, cs)
        c_a = h_dot(1, cs)
        x_a = h_dot(2, cs)
        cx = c_a * x_a
        cx_buf[0:SHORT_HALO, :] = cx_halo[:, cs]
        cx_buf[SHORT_HALO:SHORT_HALO + t, :] = cx
        cx_halo[:, cs] = cx_buf[t:t + SHORT_HALO, :]
        win = cx_buf[...]
        conv = wsh_ref[SHORT_K - 1:SHORT_K, cs] * cx
        for lag in range(1, SHORT_K):
            k = SHORT_K - 1 - lag
            conv = conv + wsh_ref[k:k + 1, cs] * pltpu.roll(win, lag, axis=0)[SHORT_HALO:, :]
        ya_buf[:, cs] = (b_a * conv).astype(BF16)

    for cs in col_blocks:
        gu_buf[:, cs] = _gelu_tanh(h_dot(3, cs)).astype(BF16)
        ln_buf[:, cs] = _gelu_tanh(h_dot(4, cs))

    for cs in col_blocks:
        for i in range(N_BRANCH):
            gate_buf[i, :, cs] = _sigmoid(h_dot(7 + i, cs)).astype(BF16)
    assert not conv_jobs

    mu, rstd = _row_stats(ln_buf[...], d)
    tri = (lax.broadcasted_iota(jnp.int32, (CHUNK, CHUNK), 0)
           >= lax.broadcasted_iota(jnp.int32, (CHUNK, CHUNK), 1))
    gd = d // N_SGU_GROUPS
    for g in range(N_SGU_GROUPS):
        gs = slice(g * gd, (g + 1) * gd)
        vn = ((ln_buf[:, gs] - mu) * rstd * lng_ref[:, gs] + lnb_ref[:, gs]).astype(BF16)
        wg = jnp.where(tri, wsgu_ref[g], 0.0).astype(BF16)
        bg = bsgu_ref[:, g:g + 1]
        for n in range(t // CHUNK):
            rs = slice(n * CHUNK, (n + 1) * CHUNK)
            mixed = _dot(wg, vn[rs, :]) + bg
            yb_buf[rs, gs] = (gu_buf[rs, gs].astype(F32) * mixed).astype(BF16)

    for cs in col_blocks:
        m = gate_buf[0, :, cs].astype(F32) * _dot(ya_buf[...], w_a_ref[:, cs])
        m = m + gate_buf[1, :, cs].astype(F32) * _dot(yb_buf[...], w_b_ref[:, cs])
        ln_buf[:, cs] = m

    mu, rstd = _row_stats(ln_c_buf[...], d)
    yc_buf[...] = _silu((ln_c_buf[...] - mu) * rstd * clg_ref[...] + clb_ref[...]).astype(BF16)
    mg_buf = gu_buf
    for cs in col_blocks:
        m = ln_buf[:, cs] + gate_buf[2, :, cs].astype(F32) * _dot(yc_buf[...], w_c_ref[:, cs])
        mg_buf[:, cs] = m.astype(BF16)
    for cs in col_blocks:
        o = _dot(mg_buf[...], w_o_ref[:, cs])
        out_ref[0, :, cs] = x_ref[0, :, cs] + mod_ref[0, 2:3, cs] * o

    glu_buf[0:CFM_HALO, :] = glu_buf[t:t + CFM_HALO, :]


def _resident(shape):
    zeros = (0,) * len(shape)
    return pl.BlockSpec(shape, lambda b, s: zeros, pipeline_mode=pl.Buffered(1))


def _mixer(x, mod, n1g, w_in, wsh, w_a, lng, lnb, wsgu, bsgu_t, w_b, cw, cb, clg, clb, w_c, w_o):
    bn, sq, d = x.shape
    t = TOK_TILE
    params = (n1g, w_in, wsh, w_a, lng, lnb, wsgu, bsgu_t, w_b, cw, cb, clg, clb, w_c, w_o)
    return pl.pallas_call(
        _mixer_kernel,
        grid=(bn, sq // t),
        in_specs=[pl.BlockSpec((1, t, d), lambda b, s: (b, s, 0)),
                  pl.BlockSpec((1, N_MOD, d), lambda b, s: (b, 0, 0))]
                 + [_resident(p.shape) for p in params],
        out_specs=pl.BlockSpec((1, t, d), lambda b, s: (b, s, 0)),
        out_shape=jax.ShapeDtypeStruct(x.shape, F32),
        scratch_shapes=[
            pltpu.VMEM((t, d), BF16),
            pltpu.VMEM((SHORT_HALO + t, COL_BLOCK), F32),
            pltpu.VMEM((SHORT_HALO, d), F32),
            pltpu.VMEM((CFM_HALO + t, d), F32),
            pltpu.VMEM((t, d), BF16),
            pltpu.VMEM((t, d), BF16),
            pltpu.VMEM((t, d), BF16),
            pltpu.VMEM((t, d), BF16),
            pltpu.VMEM((t, d), F32),
            pltpu.VMEM((t, d), F32),
            pltpu.VMEM((N_BRANCH, t, d), BF16),
        ],
        compiler_params=pltpu.CompilerParams(
            dimension_semantics=("arbitrary", "arbitrary"),
            vmem_limit_bytes=VMEM_LIMIT_BYTES),
        name="token_mixer",
    )(x, mod, *params)


def _ffn_kernel(x_ref, mod_ref, n2g_ref, w_in_ref, w_out_ref, fg_ref, out_ref, h_buf, a_buf,
                *, final_norm):
    t, d = h_buf.shape
    d_ff = a_buf.shape[1]
    h_buf[...] = _modulated_rms(x_ref[0], n2g_ref[...], mod_ref[0, 3:4, :],
                                mod_ref[0, 4:5, :]).astype(BF16)
    for j in range(0, d_ff, COL_BLOCK):
        cs = slice(j, min(j + COL_BLOCK, d_ff))
        us = slice(d_ff + cs.start, d_ff + cs.stop)
        hb = h_buf[...]
        a_buf[:, cs] = (_silu(_dot(hb, w_in_ref[:, cs])) * _dot(hb, w_in_ref[:, us])).astype(BF16)
    for j in range(0, d, COL_BLOCK):
        cs = slice(j, j + COL_BLOCK)
        o = _dot(a_buf[...], w_out_ref[:, cs])
        out_ref[0, :, cs] = x_ref[0, :, cs] + mod_ref[0, 5:6, cs] * o
    if final_norm:
        y = out_ref[0]
        ms = jnp.mean(y * y, axis=-1, keepdims=True)
        out_ref[0] = y * lax.rsqrt(ms + EPS) * fg_ref[...]


def _ffn(x, mod, n2g, w_in, w_out, fg, *, final_norm):
    bn, sq, d = x.shape
    d_ff = w_out.shape[0]
    t = TOK_TILE
    params = (n2g, w_in, w_out, fg)
    return pl.pallas_call(
        functools.partial(_ffn_kernel, final_norm=final_norm),
        grid=(bn, sq // t),
        in_specs=[pl.BlockSpec((1, t, d), lambda b, s: (b, s, 0)),
                  pl.BlockSpec((1, N_MOD, d), lambda b, s: (b, 0, 0))]
                 + [_resident(p.shape) for p in params],
        out_specs=pl.BlockSpec((1, t, d), lambda b, s: (b, s, 0)),
        out_shape=jax.ShapeDtypeStruct(x.shape, F32),
        scratch_shapes=[pltpu.VMEM((t, d), BF16), pltpu.VMEM((t, d_ff), BF16)],
        compiler_params=pltpu.CompilerParams(
            dimension_semantics=("arbitrary", "arbitrary"),
            vmem_limit_bytes=VMEM_LIMIT_BYTES),
        name="swiglu_final" if final_norm else "swiglu",
    )(x, mod, *params)


def kernel(x, c, w_ada, b_ada, norm1_g, w_in, w_short, w_a_out, sgu_ln_g, sgu_ln_b, w_sgu, b_sgu,
           w_b_out, cfm_conv_w, cfm_conv_b, cfm_ln_g, cfm_ln_b, w_c_out, w_o, norm2_g, w_ffn_in,
           w_ffn_out, final_g):
    depth = w_in.shape[0]
    d = x.shape[-1]
    assert x.shape[1] % TOK_TILE == 0 and TOK_TILE % CHUNK == 0 and d % COL_BLOCK == 0
    assert TOK_TILE % CONV_ROWS == 0 and w_sgu.shape[-1] == CHUNK
    assert w_short.shape[1] == SHORT_K and cfm_conv_w.shape[1] == CFM_K

    mod = _modulation(c, w_ada, b_ada)
    row = lambda p: p.reshape(1, -1)
    fg = row(final_g)
    for l in range(depth):
        x = _mixer(x, mod[l], row(norm1_g[l]), w_in[l].astype(BF16), w_short[l],
                   w_a_out[l].astype(BF16), row(sgu_ln_g[l]), row(sgu_ln_b[l]), w_sgu[l],
                   b_sgu[l].T, w_b_out[l].astype(BF16), cfm_conv_w[l], row(cfm_conv_b[l]),
                   row(cfm_ln_g[l]), row(cfm_ln_b[l]), w_c_out[l].astype(BF16),
                   w_o[l].astype(BF16))
        x = _ffn(x, mod[l], row(norm2_g[l]), w_ffn_in[l].astype(BF16),
                 w_ffn_out[l].astype(BF16), fg, final_norm=(l == depth - 1))
    return x
```

```python
import functools

import jax
import jax.numpy as jnp
from jax import lax
from jax.experimental import pallas as pl
from jax.experimental.pallas import tpu as pltpu

EPS = 1e-6
CHUNK = 128
N_SGU_GROUPS = 8
SHORT_K = 3
CFM_K = 31
N_MOD = 6
N_BRANCH = 3

TOK_TILE = 512
COL_BLOCK = 512
CONV_ROWS = 64
SUB_BLOCK = 256
SUBS_PER_DOT = 2
W_IN_GROUPS = {"short": (0, 3), "sgu": (3, 2), "glu": (5, 2), "gates": (7, 3)}
SUBLANES = 8
LANES = 128
SHORT_HALO = 8
CFM_HALO = 32
VMEM_LIMIT_BYTES = 63 * 1024 * 1024

F32 = jnp.float32
BF16 = jnp.bfloat16


def _dot(a, b):
    return jnp.dot(a, b, preferred_element_type=F32)


def _sigmoid(x):
    return 1.0 / (1.0 + jnp.exp(-x))


def _silu(x):
    return x * _sigmoid(x)


def _gelu_tanh(x):
    c = 0.7978845608028654
    return 0.5 * x * (1.0 + jnp.tanh(c * (x + 0.044715 * (x * x * x))))


def _row_stats(v, n):
    mu = jnp.sum(v, axis=-1, keepdims=True) * (1.0 / n)
    ex2 = jnp.sum(v * v, axis=-1, keepdims=True) * (1.0 / n)
    return mu, lax.rsqrt(ex2 - mu * mu + EPS)


def _modulated_rms(x, g, shift, scale):
    ms = jnp.mean(x * x, axis=-1, keepdims=True)
    return (x * lax.rsqrt(ms + EPS) * g) * (1.0 + scale) + shift


def _causal_conv_rows(win, w_ref, ls):
    k_taps = w_ref.shape[0]
    rows = win.shape[0] - CFM_HALO
    out = None
    for res in range(SUBLANES):
        lead = 0 if res == 0 else SUBLANES
        part = None
        for q in range((k_taps - 1 - res) // SUBLANES + 1):
            lag = SUBLANES * q + res
            lo = CFM_HALO - lead - SUBLANES * q
            term = w_ref[k_taps - 1 - lag:k_taps - lag, ls] * win[lo:lo + lead + rows, :]
            part = term if part is None else part + term
        if res == 0:
            out = part
        else:
            out = out + pltpu.roll(part, res, axis=0)[lead:, :]
    return out


def _mod_kernel(c_ref, w_ref, b_ref, o_ref):
    c = c_ref[...]
    o_ref[0] = jnp.dot(_silu(c), w_ref[0], preferred_element_type=F32,
                       precision=lax.Precision.HIGHEST) + b_ref[0]


def _modulation(c, w_ada, b_ada):
    depth, d, n = w_ada.shape
    bn = c.shape[0]
    nb = n // 4
    out = pl.pallas_call(
        _mod_kernel,
        grid=(depth, n // nb),
        in_specs=[
            pl.BlockSpec((bn, d), lambda l, j: (0, 0)),
            pl.BlockSpec((1, d, nb), lambda l, j: (l, 0, j)),
            pl.BlockSpec((1, 1, nb), lambda l, j: (l, 0, j)),
        ],
        out_specs=pl.BlockSpec((1, bn, nb), lambda l, j: (l, 0, j)),
        out_shape=jax.ShapeDtypeStruct((depth, bn, n), F32),
        name="adaln_modulation",
    )(c, w_ada, b_ada.reshape(depth, 1, n))
    return out.reshape(depth, bn, N_MOD, d)


def _mixer_kernel(x_ref, mod_ref, n1g_ref, w_in_ref, wsh_ref, w_a_ref, lng_ref, lnb_ref,
                  wsgu_ref, bsgu_ref, w_b_ref, cw_ref, cb_ref, clg_ref, clb_ref, w_c_ref,
                  w_o_ref, out_ref,
                  h_buf, cx_buf, cx_halo, glu_buf, ya_buf, yb_buf, gu_buf, ln_buf,
                  ln_c_buf, gate_buf):
    t, d = h_buf.shape
    s = pl.program_id(1)

    @pl.when(s == 0)
    def _():
        cx_halo[...] = jnp.zeros((SHORT_HALO, d), F32)
        glu_buf[0:CFM_HALO, :] = jnp.zeros((CFM_HALO, d), F32)

    col_blocks = [slice(j, j + COL_BLOCK) for j in range(0, d, COL_BLOCK)]

    conv_jobs = [(r0, l0) for l0 in range(0, d, LANES) for r0 in range(0, t, CONV_ROWS)]
    jobs_per_slot = -(-len(conv_jobs) // (3 * (d // SUB_BLOCK // SUBS_PER_DOT)))

    def run_conv_jobs(n):
        for _ in range(min(n, len(conv_jobs))):
            r0, l0 = conv_jobs.pop(0)
            ls = slice(l0, l0 + LANES)
            ln_c_buf[r0:r0 + CONV_ROWS, ls] = _causal_conv_rows(
                glu_buf[r0:r0 + CFM_HALO + CONV_ROWS, ls], cw_ref, ls) + cb_ref[:, ls]

    def h_dots(group):
        lo, width = W_IN_GROUPS[group]
        n = SUBS_PER_DOT * width * SUB_BLOCK
        for j0 in range(0, d // SUB_BLOCK, SUBS_PER_DOT):
            c0 = lo * d + j0 * width * SUB_BLOCK
            yield j0, (_dot(h_buf[...], w_in_ref[:, c0:c0 + n]), width)

    def part(res_width, i, jj):
        res, width = res_width
        c0 = (jj * width + i) * SUB_BLOCK
        return res[:, c0:c0 + SUB_BLOCK]

    def sub_blocks(j0):
        for jj in range(SUBS_PER_DOT):
            yield jj, slice((j0 + jj) * SUB_BLOCK, (j0 + jj + 1) * SUB_BLOCK)

    h_buf[...] = _modulated_rms(x_ref[0], n1g_ref[...], mod_ref[0, 0:1, :],
                                mod_ref[0, 1:2, :]).astype(BF16)

    for j0, res in h_dots("glu"):
        for jj, cs in sub_blocks(j0):
            glu_buf[CFM_HALO:CFM_HALO + t, cs] = part(res, 0, jj) * _sigmoid(part(res, 1, jj))

    for j0, res in h_dots("gates"):
        run_conv_jobs(jobs_per_slot)
        for jj, cs in sub_blocks(j0):
            for i in range(N_BRANCH):
                gate_buf[i, :, cs] = _sigmoid(part(res, i, jj)).astype(BF16)

    for j0, res in h_dots("short"):
        run_conv_jobs(jobs_per_slot)
        for jj, cs in sub_blocks(j0):
            cx = part(res, 1, jj) * part(res, 2, jj)
            cx_buf[0:SHORT_HALO, :] = cx_halo[:, cs]
            cx_buf[SHORT_HALO:SHORT_HALO + t, :] = cx
            cx_halo[:, cs] = cx_buf[t:t + SHORT_HALO, :]
            win = cx_buf[...]
            conv = wsh_ref[SHORT_K - 1:SHORT_K, cs] * cx
            for lag in range(1, SHORT_K):
                k = SHORT_K - 1 - lag
                conv = conv + wsh_ref[k:k + 1, cs] * pltpu.roll(win, lag, axis=0)[SHORT_HALO:, :]
            ya_buf[:, cs] = (part(res, 0, jj) * conv).astype(BF16)

    for j0, res in h_dots("sgu"):
        run_conv_jobs(jobs_per_slot)
        for jj, cs in sub_blocks(j0):
            gu_buf[:, cs] = _gelu_tanh(part(res, 0, jj)).astype(BF16)
            ln_buf[:, cs] = _gelu_tanh(part(res, 1, jj))
    assert not conv_jobs

    mu, rstd = _row_stats(ln_buf[...], d)
    tri = (lax.broadcasted_iota(jnp.int32, (CHUNK, CHUNK), 0)
           >= lax.broadcasted_iota(jnp.int32, (CHUNK, CHUNK), 1))
    gd = d // N_SGU_GROUPS
    chunks = [slice(n * CHUNK, (n + 1) * CHUNK) for n in range(t // CHUNK)]
    for g in range(N_SGU_GROUPS):
        gs = slice(g * gd, (g + 1) * gd)
        vn = ((ln_buf[:, gs] - mu) * rstd * lng_ref[:, gs] + lnb_ref[:, gs]).astype(BF16)
        wg = jnp.where(tri, wsgu_ref[g].astype(F32), 0.0).astype(BF16)
        mixed = _dot(wg, jnp.concatenate([vn[rs, :] for rs in chunks], axis=1))
        bg = bsgu_ref[:, g:g + 1]
        for n, rs in enumerate(chunks):
            yb_buf[rs, gs] = (gu_buf[rs, gs].astype(F32)
                              * (mixed[:, n * gd:(n + 1) * gd] + bg)).astype(BF16)

    for cs in col_blocks:
        m = gate_buf[0, :, cs].astype(F32) * _dot(ya_buf[...], w_a_ref[:, cs])
        m = m + gate_buf[1, :, cs].astype(F32) * _dot(yb_buf[...], w_b_ref[:, cs])
        ln_buf[:, cs] = m

    mu, rstd = _row_stats(ln_c_buf[...], d)
    yc_buf = ya_buf
    yc_buf[...] = _silu((ln_c_buf[...] - mu) * rstd * clg_ref[...] + clb_ref[...]).astype(BF16)
    mg_buf = gu_buf
    for cs in col_blocks:
        m = ln_buf[:, cs] + gate_buf[2, :, cs].astype(F32) * _dot(yc_buf[...], w_c_ref[:, cs])
        mg_buf[:, cs] = m.astype(BF16)
    for cs in col_blocks:
        o = _dot(mg_buf[...], w_o_ref[:, cs])
        out_ref[0, :, cs] = x_ref[0, :, cs] + mod_ref[0, 2:3, cs] * o

    glu_buf[0:CFM_HALO, :] = glu_buf[t:t + CFM_HALO, :]


def _resident(shape):
    zeros = (0,) * len(shape)
    return pl.BlockSpec(shape, lambda b, s: zeros, pipeline_mode=pl.Buffered(1))


def _mixer(x, mod, n1g, w_in, wsh, w_a, lng, lnb, wsgu, bsgu_t, w_b, cw, cb, clg, clb, w_c, w_o):
    bn, sq, d = x.shape
    t = TOK_TILE
    params = (n1g, w_in, wsh, w_a, lng, lnb, wsgu, bsgu_t, w_b, cw, cb, clg, clb, w_c, w_o)
    return pl.pallas_call(
        _mixer_kernel,
        grid=(bn, sq // t),
        in_specs=[pl.BlockSpec((1, t, d), lambda b, s: (b, s, 0)),
                  pl.BlockSpec((1, N_MOD, d), lambda b, s: (b, 0, 0))]
                 + [_resident(p.shape) for p in params],
        out_specs=pl.BlockSpec((1, t, d), lambda b, s: (b, s, 0)),
        out_shape=jax.ShapeDtypeStruct(x.shape, F32),
        scratch_shapes=[
            pltpu.VMEM((t, d), BF16),
            pltpu.VMEM((SHORT_HALO + t, SUB_BLOCK), F32),
            pltpu.VMEM((SHORT_HALO, d), F32),
            pltpu.VMEM((CFM_HALO + t, d), F32),
            pltpu.VMEM((t, d), BF16),
            pltpu.VMEM((t, d), BF16),
            pltpu.VMEM((t, d), BF16),
            pltpu.VMEM((t, d), F32),
            pltpu.VMEM((t, d), F32),
            pltpu.VMEM((N_BRANCH, t, d), BF16),
        ],
        compiler_params=pltpu.CompilerParams(
            dimension_semantics=("arbitrary", "arbitrary"),
            vmem_limit_bytes=VMEM_LIMIT_BYTES),
        name="token_mixer",
    )(x, mod, *params)


def _ffn_kernel(x_ref, mod_ref, n2g_ref, w_in_ref, w_out_ref, fg_ref, out_ref, h_buf, a_buf,
                *, final_norm):
    t, d = h_buf.shape
    d_ff = a_buf.shape[1]
    h_buf[...] = _modulated_rms(x_ref[0], n2g_ref[...], mod_ref[0, 3:4, :],
                                mod_ref[0, 4:5, :]).astype(BF16)
    for j in range(0, d_ff, COL_BLOCK):
        cs = slice(j, min(j + COL_BLOCK, d_ff))
        us = slice(d_ff + cs.start, d_ff + cs.stop)
        hb = h_buf[...]
        a_buf[:, cs] = (_silu(_dot(hb, w_in_ref[:, cs])) * _dot(hb, w_in_ref[:, us])).astype(BF16)
    for j in range(0, d, COL_BLOCK):
        cs = slice(j, j + COL_BLOCK)
        o = _dot(a_buf[...], w_out_ref[:, cs])
        out_ref[0, :, cs] = x_ref[0, :, cs] + mod_ref[0, 5:6, cs] * o
    if final_norm:
        y = out_ref[0]
        ms = jnp.mean(y * y, axis=-1, keepdims=True)
        out_ref[0] = y * lax.rsqrt(ms + EPS) * fg_ref[...]


def _ffn(x, mod, n2g, w_in, w_out, fg, *, final_norm):
    bn, sq, d = x.shape
    d_ff = w_out.shape[0]
    t = TOK_TILE
    params = (n2g, w_in, w_out, fg)
    return pl.pallas_call(
        functools.partial(_ffn_kernel, final_norm=final_norm),
        grid=(bn, sq // t),
        in_specs=[pl.BlockSpec((1, t, d), lambda b, s: (b, s, 0)),
                  pl.BlockSpec((1, N_MOD, d), lambda b, s: (b, 0, 0))]
                 + [_resident(p.shape) for p in params],
        out_specs=pl.BlockSpec((1, t, d), lambda b, s: (b, s, 0)),
        out_shape=jax.ShapeDtypeStruct(x.shape, F32),
        scratch_shapes=[pltpu.VMEM((t, d), BF16), pltpu.VMEM((t, d_ff), BF16)],
        compiler_params=pltpu.CompilerParams(
            dimension_semantics=("arbitrary", "arbitrary"),
            vmem_limit_bytes=VMEM_LIMIT_BYTES),
        name="swiglu_final" if final_norm else "swiglu",
    )(x, mod, *params)


def _interleave_w_in(w):
    d = w.shape[0]
    out = []
    for lo, width in W_IN_GROUPS.values():
        g = w[:, lo * d:(lo + width) * d].astype(BF16).reshape(d, width, d // SUB_BLOCK, SUB_BLOCK)
        out.append(jnp.swapaxes(g, 1, 2).reshape(d, width * d))
    return jnp.concatenate(out, axis=1)


def kernel(x, c, w_ada, b_ada, norm1_g, w_in, w_short, w_a_out, sgu_ln_g, sgu_ln_b, w_sgu, b_sgu,
           w_b_out, cfm_conv_w, cfm_conv_b, cfm_ln_g, cfm_ln_b, w_c_out, w_o, norm2_g, w_ffn_in,
           w_ffn_out, final_g):
    depth = w_in.shape[0]
    d = x.shape[-1]
    assert x.shape[1] % TOK_TILE == 0 and TOK_TILE % CHUNK == 0 and d % COL_BLOCK == 0
    assert TOK_TILE % CONV_ROWS == 0 and w_sgu.shape[-1] == CHUNK
    assert w_short.shape[1] == SHORT_K and cfm_conv_w.shape[1] == CFM_K

    mod = _modulation(c, w_ada, b_ada)
    row = lambda p: p.reshape(1, -1)
    fg = row(final_g)
    for l in range(depth):
        x = _mixer(x, mod[l], row(norm1_g[l]), _interleave_w_in(w_in[l]), w_short[l],
                   w_a_out[l].astype(BF16), row(sgu_ln_g[l]), row(sgu_ln_b[l]), w_sgu[l].astype(BF16),
                   b_sgu[l].T, w_b_out[l].astype(BF16), cfm_conv_w[l], row(cfm_conv_b[l]),
                   row(cfm_ln_g[l]), row(cfm_ln_b[l]), w_c_out[l].astype(BF16),
                   w_o[l].astype(BF16))
        x = _ffn(x, mod[l], row(norm2_g[l]), w_ffn_in[l].astype(BF16),
                 w_ffn_out[l].astype(BF16), fg, final_norm=(l == depth - 1))
    return x
```

```python
import functools

import jax
import jax.numpy as jnp
from jax import lax
from jax.experimental import pallas as pl
from jax.experimental.pallas import tpu as pltpu

EPS = 1e-6
CHUNK = 128
N_SGU_GROUPS = 8
SHORT_K = 3
CFM_K = 31
N_MOD = 6
N_BRANCH = 3

TOK_TILE = 512
COL_BLOCK = 512
CONV_ROWS = 64
SUB_BLOCK = 256
SUBS_PER_DOT = 2
W_IN_GROUPS = {"short": (0, 3), "sgu": (3, 2), "glu": (5, 2), "gates": (7, 3)}
SUBLANES = 8
LANES = 128
SHORT_HALO = 8
CFM_HALO = 32
VMEM_LIMIT_BYTES = 63 * 1024 * 1024

F32 = jnp.float32
BF16 = jnp.bfloat16


def _dot(a, b):
    return jnp.dot(a, b, preferred_element_type=F32)


def _sigmoid(x):
    return 1.0 / (1.0 + jnp.exp(-x))


def _silu(x):
    return x * _sigmoid(x)


def _gelu_tanh(x):
    c = 0.7978845608028654
    return 0.5 * x * (1.0 + jnp.tanh(c * (x + 0.044715 * (x * x * x))))


def _row_stats(v, n):
    mu = jnp.sum(v, axis=-1, keepdims=True) * (1.0 / n)
    ex2 = jnp.sum(v * v, axis=-1, keepdims=True) * (1.0 / n)
    return mu, lax.rsqrt(ex2 - mu * mu + EPS)


def _modulated_rms(x, g, shift, scale):
    ms = jnp.mean(x * x, axis=-1, keepdims=True)
    return (x * lax.rsqrt(ms + EPS) * g) * (1.0 + scale) + shift


def _causal_conv_rows(win, w_ref, ls):
    k_taps = w_ref.shape[0]
    rows = win.shape[0] - CFM_HALO
    out = None
    for res in range(SUBLANES):
        lead = 0 if res == 0 else SUBLANES
        part = None
        for q in range((k_taps - 1 - res) // SUBLANES + 1):
            lag = SUBLANES * q + res
            lo = CFM_HALO - lead - SUBLANES * q
            term = w_ref[k_taps - 1 - lag:k_taps - lag, ls] * win[lo:lo + lead + rows, :]
            part = term if part is None else part + term
        if res == 0:
            out = part
        else:
            out = out + pltpu.roll(part, res, axis=0)[lead:, :]
    return out


def _mod_kernel(c_ref, w_ref, b_ref, o_ref):
    c = c_ref[...]
    o_ref[0] = jnp.dot(_silu(c), w_ref[0], preferred_element_type=F32,
                       precision=lax.Precision.HIGHEST) + b_ref[0]


def _modulation(c, w_ada, b_ada):
    depth, d, n = w_ada.shape
    bn = c.shape[0]
    nb = n // 4
    out = pl.pallas_call(
        _mod_kernel,
        grid=(depth, n // nb),
        in_specs=[
            pl.BlockSpec((bn, d), lambda l, j: (0, 0)),
            pl.BlockSpec((1, d, nb), lambda l, j: (l, 0, j)),
            pl.BlockSpec((1, 1, nb), lambda l, j: (l, 0, j)),
        ],
        out_specs=pl.BlockSpec((1, bn, nb), lambda l, j: (l, 0, j)),
        out_shape=jax.ShapeDtypeStruct((depth, bn, n), F32),
        name="adaln_modulation",
    )(c, w_ada, b_ada.reshape(depth, 1, n))
    return out.reshape(depth, bn, N_MOD, d)


def _mixer_kernel(x_ref, mod_ref, n1g_ref, w_in_ref, wsh_ref, w_a_ref, lng_ref, lnb_ref,
                  wsgu_ref, bsgu_ref, w_b_ref, cw_ref, cb_ref, clg_ref, clb_ref, w_c_ref,
                  w_o_ref, out_ref,
                  h_buf, cx_buf, cx_halo, glu_buf, ya_buf, yb_buf, gu_buf, ln_buf,
                  ln_c_buf, gate_buf):
    t, d = h_buf.shape
    s = pl.program_id(1)

    @pl.when(s == 0)
    def _():
        cx_halo[...] = jnp.zeros((SHORT_HALO, d), F32)
        glu_buf[0:CFM_HALO, :] = jnp.zeros((CFM_HALO, d), F32)

    col_blocks = [slice(j, j + COL_BLOCK) for j in range(0, d, COL_BLOCK)]

    conv_jobs = [(r0, l0) for l0 in range(0, d, LANES) for r0 in range(0, t, CONV_ROWS)]
    jobs_per_slot = -(-len(conv_jobs) // (3 * (d // SUB_BLOCK // SUBS_PER_DOT)))

    def run_conv_jobs(n):
        for _ in range(min(n, len(conv_jobs))):
            r0, l0 = conv_jobs.pop(0)
            ls = slice(l0, l0 + LANES)
            ln_c_buf[r0:r0 + CONV_ROWS, ls] = _causal_conv_rows(
                glu_buf[r0:r0 + CFM_HALO + CONV_ROWS, ls], cw_ref, ls) + cb_ref[:, ls]

    def h_dots(group):
        lo, width = W_IN_GROUPS[group]
        for j0 in range(0, d // SUB_BLOCK, SUBS_PER_DOT):
            cols = [(lo + i) * d + (j0 + jj) * SUB_BLOCK
                    for jj in range(SUBS_PER_DOT) for i in range(width)]
            rhs = jnp.concatenate([w_in_ref[:, c:c + SUB_BLOCK] for c in cols], axis=1)
            yield j0, (_dot(h_buf[...], rhs), width)

    def part(res_width, i, jj):
        res, width = res_width
        c0 = (jj * width + i) * SUB_BLOCK
        return res[:, c0:c0 + SUB_BLOCK]

    def sub_blocks(j0):
        for jj in range(SUBS_PER_DOT):
            yield jj, slice((j0 + jj) * SUB_BLOCK, (j0 + jj + 1) * SUB_BLOCK)

    h_buf[...] = _modulated_rms(x_ref[0], n1g_ref[...], mod_ref[0, 0:1, :],
                                mod_ref[0, 1:2, :]).astype(BF16)

    for j0, res in h_dots("glu"):
        for jj, cs in sub_blocks(j0):
            glu_buf[CFM_HALO:CFM_HALO + t, cs] = part(res, 0, jj) * _sigmoid(part(res, 1, jj))

    for j0, res in h_dots("gates"):
        run_conv_jobs(jobs_per_slot)
        for jj, cs in sub_blocks(j0):
            for i in range(N_BRANCH):
                gate_buf[i, :, cs] = _sigmoid(part(res, i, jj)).astype(BF16)

    for j0, res in h_dots("short"):
        run_conv_jobs(jobs_per_slot)
        for jj, cs in sub_blocks(j0):
            cx = part(res, 1, jj) * part(res, 2, jj)
            cx_buf[0:SHORT_HALO, :] = cx_halo[:, cs]
            cx_buf[SHORT_HALO:SHORT_HALO + t, :] = cx
            cx_halo[:, cs] = cx_buf[t:t + SHORT_HALO, :]
            win = cx_buf[...]
            conv = wsh_ref[SHORT_K - 1:SHORT_K, cs] * cx
            for lag in range(1, SHORT_K):
                k = SHORT_K - 1 - lag
                conv = conv + wsh_ref[k:k + 1, cs] * pltpu.roll(win, lag, axis=0)[SHORT_HALO:, :]
            ya_buf[:, cs] = (part(res, 0, jj) * conv).astype(BF16)

    for j0, res in h_dots("sgu"):
        run_conv_jobs(jobs_per_slot)
        for jj, cs in sub_blocks(j0):
            gu_buf[:, cs] = _gelu_tanh(part(res, 0, jj)).astype(BF16)
            ln_buf[:, cs] = _gelu_tanh(part(res, 1, jj))
    assert not conv_jobs

    mu, rstd = _row_stats(ln_buf[...], d)
    tri = (lax.broadcasted_iota(jnp.int32, (CHUNK, CHUNK), 0)
           >= lax.broadcasted_iota(jnp.int32, (CHUNK, CHUNK), 1))
    gd = d // N_SGU_GROUPS
    chunks = [slice(n * CHUNK, (n + 1) * CHUNK) for n in range(t // CHUNK)]
    for g in range(N_SGU_GROUPS):
        gs = slice(g * gd, (g + 1) * gd)
        vn = ((ln_buf[:, gs] - mu) * rstd * lng_ref[:, gs] + lnb_ref[:, gs]).astype(BF16)
        wg = jnp.where(tri, wsgu_ref[g].astype(F32), 0.0).astype(BF16)
        mixed = _dot(wg, jnp.concatenate([vn[rs, :] for rs in chunks], axis=1))
        bg = bsgu_ref[:, g:g + 1]
        for n, rs in enumerate(chunks):
            yb_buf[rs, gs] = (gu_buf[rs, gs].astype(F32)
                              * (mixed[:, n * gd:(n + 1) * gd] + bg)).astype(BF16)

    for cs in col_blocks:
        m = gate_buf[0, :, cs].astype(F32) * _dot(ya_buf[...], w_a_ref[:, cs])
        m = m + gate_buf[1, :, cs].astype(F32) * _dot(yb_buf[...], w_b_ref[:, cs])
        ln_buf[:, cs] = m

    mu, rstd = _row_stats(ln_c_buf[...], d)
    yc_buf = ya_buf
    yc_buf[...] = _silu((ln_c_buf[...] - mu) * rstd * clg_ref[...] + clb_ref[...]).astype(BF16)
    mg_buf = gu_buf
    for cs in col_blocks:
        m = ln_buf[:, cs] + gate_buf[2, :, cs].astype(F32) * _dot(yc_buf[...], w_c_ref[:, cs])
        mg_buf[:, cs] = m.astype(BF16)
    for cs in col_blocks:
        o = _dot(mg_buf[...], w_o_ref[:, cs])
        out_ref[0, :, cs] = x_ref[0, :, cs] + mod_ref[0, 2:3, cs] * o

    glu_buf[0:CFM_HALO, :] = glu_buf[t:t + CFM_HALO, :]


def _resident(shape):
    zeros = (0,) * len(shape)
    return pl.BlockSpec(shape, lambda b, s: zeros, pipeline_mode=pl.Buffered(1))


def _mixer(x, mod, n1g, w_in, wsh, w_a, lng, lnb, wsgu, bsgu_t, w_b, cw, cb, clg, clb, w_c, w_o):
    bn, sq, d = x.shape
    t = TOK_TILE
    params = (n1g, w_in, wsh, w_a, lng, lnb, wsgu, bsgu_t, w_b, cw, cb, clg, clb, w_c, w_o)
    return pl.pallas_call(
        _mixer_kernel,
        grid=(bn, sq // t),
        in_specs=[pl.BlockSpec((1, t, d), lambda b, s: (b, s, 0)),
                  pl.BlockSpec((1, N_MOD, d), lambda b, s: (b, 0, 0))]
                 + [_resident(p.shape) for p in params],
        out_specs=pl.BlockSpec((1, t, d), lambda b, s: (b, s, 0)),
        out_shape=jax.ShapeDtypeStruct(x.shape, F32),
        scratch_shapes=[
            pltpu.VMEM((t, d), BF16),
            pltpu.VMEM((SHORT_HALO + t, SUB_BLOCK), F32),
            pltpu.VMEM((SHORT_HALO, d), F32),
            pltpu.VMEM((CFM_HALO + t, d), F32),
            pltpu.VMEM((t, d), BF16),
            pltpu.VMEM((t, d), BF16),
            pltpu.VMEM((t, d), BF16),
            pltpu.VMEM((t, d), F32),
            pltpu.VMEM((t, d), F32),
            pltpu.VMEM((N_BRANCH, t, d), BF16),
        ],
        compiler_params=pltpu.CompilerParams(
            dimension_semantics=("arbitrary", "arbitrary"),
            vmem_limit_bytes=VMEM_LIMIT_BYTES),
        name="token_mixer",
    )(x, mod, *params)


def _ffn_kernel(x_ref, mod_ref, n2g_ref, w_in_ref, w_out_ref, fg_ref, out_ref, h_buf, a_buf,
                *, final_norm):
    t, d = h_buf.shape
    d_ff = a_buf.shape[1]
    h_buf[...] = _modulated_rms(x_ref[0], n2g_ref[...], mod_ref[0, 3:4, :],
                                mod_ref[0, 4:5, :]).astype(BF16)
    for j in range(0, d_ff, COL_BLOCK):
        cs = slice(j, min(j + COL_BLOCK, d_ff))
        us = slice(d_ff + cs.start, d_ff + cs.stop)
        hb = h_buf[...]
        a_buf[:, cs] = (_silu(_dot(hb, w_in_ref[:, cs])) * _dot(hb, w_in_ref[:, us])).astype(BF16)
    for j in range(0, d, COL_BLOCK):
        cs = slice(j, j + COL_BLOCK)
        o = _dot(a_buf[...], w_out_ref[:, cs])
        out_ref[0, :, cs] = x_ref[0, :, cs] + mod_ref[0, 5:6, cs] * o
    if final_norm:
        y = out_ref[0]
        ms = jnp.mean(y * y, axis=-1, keepdims=True)
        out_ref[0] = y * lax.rsqrt(ms + EPS) * fg_ref[...]


def _ffn(x, mod, n2g, w_in, w_out, fg, *, final_norm):
    bn, sq, d = x.shape
    d_ff = w_out.shape[0]
    t = TOK_TILE
    params = (n2g, w_in, w_out, fg)
    return pl.pallas_call(
        functools.partial(_ffn_kernel, final_norm=final_norm),
        grid=(bn, sq // t),
        in_specs=[pl.BlockSpec((1, t, d), lambda b, s: (b, s, 0)),
                  pl.BlockSpec((1, N_MOD, d), lambda b, s: (b, 0, 0))]
                 + [_resident(p.shape) for p in params],
        out_specs=pl.BlockSpec((1, t, d), lambda b, s: (b, s, 0)),
        out_shape=jax.ShapeDtypeStruct(x.shape, F32),
        scratch_shapes=[pltpu.VMEM((t, d), BF16), pltpu.VMEM((t, d_ff), BF16)],
        compiler_params=pltpu.CompilerParams(
            dimension_semantics=("arbitrary", "arbitrary"),
            vmem_limit_bytes=VMEM_LIMIT_BYTES),
        name="swiglu_final" if final_norm else "swiglu",
    )(x, mod, *params)


def kernel(x, c, w_ada, b_ada, norm1_g, w_in, w_short, w_a_out, sgu_ln_g, sgu_ln_b, w_sgu, b_sgu,
           w_b_out, cfm_conv_w, cfm_conv_b, cfm_ln_g, cfm_ln_b, w_c_out, w_o, norm2_g, w_ffn_in,
           w_ffn_out, final_g):
    depth = w_in.shape[0]
    d = x.shape[-1]
    assert x.shape[1] % TOK_TILE == 0 and TOK_TILE % CHUNK == 0 and d % COL_BLOCK == 0
    assert TOK_TILE % CONV_ROWS == 0 and w_sgu.shape[-1] == CHUNK
    assert w_short.shape[1] == SHORT_K and cfm_conv_w.shape[1] == CFM_K

    mod = _modulation(c, w_ada, b_ada)
    row = lambda p: p.reshape(1, -1)
    fg = row(final_g)
    for l in range(depth):
        x = _mixer(x, mod[l], row(norm1_g[l]), w_in[l].astype(BF16), w_short[l],
                   w_a_out[l].astype(BF16), row(sgu_ln_g[l]), row(sgu_ln_b[l]), w_sgu[l].astype(BF16),
                   b_sgu[l].T, w_b_out[l].astype(BF16), cfm_conv_w[l], row(cfm_conv_b[l]),
                   row(cfm_ln_g[l]), row(cfm_ln_b[l]), w_c_out[l].astype(BF16),
                   w_o[l].astype(BF16))
        x = _ffn(x, mod[l], row(norm2_g[l]), w_ffn_in[l].astype(BF16),
                 w_ffn_out[l].astype(BF16), fg, final_norm=(l == depth - 1))
    return x
```

```python
import functools

import jax
import jax.numpy as jnp
from jax import lax
from jax.experimental import pallas as pl
from jax.experimental.pallas import tpu as pltpu

EPS = 1e-6
CHUNK = 128
N_SGU_GROUPS = 8
SHORT_K = 3
CFM_K = 31
N_MOD = 6
N_BRANCH = 3

TOK_TILE = 512
COL_BLOCK = 512
CONV_ROWS = 64
SUB_BLOCK = 256
SUBS_PER_DOT = 2
W_IN_GROUPS = {"short": (0, 3), "sgu": (3, 2), "glu": (5, 2), "gates": (7, 3)}
FFN_LOAD_CHUNKS = 8
SUBLANES = 8
LANES = 128
SHORT_HALO = 8
CFM_HALO = 32
VMEM_LIMIT_BYTES = 63 * 1024 * 1024

F32 = jnp.float32
BF16 = jnp.bfloat16


def _dot(a, b):
    return jnp.dot(a, b, preferred_element_type=F32)


def _sigmoid(x):
    return 1.0 / (1.0 + jnp.exp(-x))


def _silu(x):
    return x * _sigmoid(x)


def _gelu_tanh(x):
    c = 0.7978845608028654
    return 0.5 * x * (1.0 + jnp.tanh(c * (x + 0.044715 * (x * x * x))))


def _row_stats(v, n):
    mu = jnp.sum(v, axis=-1, keepdims=True) * (1.0 / n)
    ex2 = jnp.sum(v * v, axis=-1, keepdims=True) * (1.0 / n)
    return mu, lax.rsqrt(ex2 - mu * mu + EPS)


def _modulated_rms(x, g, shift, scale):
    ms = jnp.mean(x * x, axis=-1, keepdims=True)
    return (x * lax.rsqrt(ms + EPS) * g) * (1.0 + scale) + shift


def _causal_conv_rows(win, w_ref, ls):
    k_taps = w_ref.shape[0]
    rows = win.shape[0] - CFM_HALO
    out = None
    for res in range(SUBLANES):
        lead = 0 if res == 0 else SUBLANES
        part = None
        for q in range((k_taps - 1 - res) // SUBLANES + 1):
            lag = SUBLANES * q + res
            lo = CFM_HALO - lead - SUBLANES * q
            term = w_ref[k_taps - 1 - lag:k_taps - lag, ls] * win[lo:lo + lead + rows, :]
            part = term if part is None else part + term
        if res == 0:
            out = part
        else:
            out = out + pltpu.roll(part, res, axis=0)[lead:, :]
    return out


def _mod_kernel(c_ref, w_ref, b_ref, o_ref):
    c = c_ref[...]
    o_ref[0] = jnp.dot(_silu(c), w_ref[0], preferred_element_type=F32,
                       precision=lax.Precision.HIGHEST) + b_ref[0]


def _modulation(c, w_ada, b_ada):
    depth, d, n = w_ada.shape
    bn = c.shape[0]
    nb = n // 4
    out = pl.pallas_call(
        _mod_kernel,
        grid=(depth, n // nb),
        in_specs=[
            pl.BlockSpec((bn, d), lambda l, j: (0, 0)),
            pl.BlockSpec((1, d, nb), lambda l, j: (l, 0, j)),
            pl.BlockSpec((1, 1, nb), lambda l, j: (l, 0, j)),
        ],
        out_specs=pl.BlockSpec((1, bn, nb), lambda l, j: (l, 0, j)),
        out_shape=jax.ShapeDtypeStruct((depth, bn, n), F32),
        name="adaln_modulation",
    )(c, w_ada, b_ada.reshape(depth, 1, n))
    return out.reshape(depth, bn, N_MOD, d)


def _is_first_step():
    return jnp.logical_and(pl.program_id(0) == 0, pl.program_id(1) == 0)


def _load_weights_bf16(jobs, stage, sem):
    def copy(k):
        return pltpu.make_async_copy(jobs[k][0], stage[k % 2], sem.at[k % 2])

    copy(0).start()
    for k, (_, dst) in enumerate(jobs):
        if k + 1 < len(jobs):
            copy(k + 1).start()
        copy(k).wait()
        dst[...] = stage[k % 2][...].astype(BF16)


def _mixer_kernel(x_ref, mod_ref, n1g_ref, w_in_ref, wsh_ref, w_a_ref, lng_ref, lnb_ref,
                  wsgu_ref, bsgu_ref, w_b_ref, cw_ref, cb_ref, clg_ref, clb_ref, w_c_ref,
                  w_o_ref, out_ref,
                  h_buf, cx_buf, cx_halo, glu_buf, ya_buf, yb_buf, gu_buf, ln_buf,
                  ln_c_buf, gate_buf):
    t, d = h_buf.shape
    s = pl.program_id(1)

    @pl.when(s == 0)
    def _():
        cx_halo[...] = jnp.zeros((SHORT_HALO, d), F32)
        glu_buf[0:CFM_HALO, :] = jnp.zeros((CFM_HALO, d), F32)

    col_blocks = [slice(j, j + COL_BLOCK) for j in range(0, d, COL_BLOCK)]

    conv_jobs = [(r0, l0) for l0 in range(0, d, LANES) for r0 in range(0, t, CONV_ROWS)]
    jobs_per_slot = -(-len(conv_jobs) // (3 * (d // SUB_BLOCK // SUBS_PER_DOT)))

    def run_conv_jobs(n):
        for _ in range(min(n, len(conv_jobs))):
            r0, l0 = conv_jobs.pop(0)
            ls = slice(l0, l0 + LANES)
            ln_c_buf[r0:r0 + CONV_ROWS, ls] = _causal_conv_rows(
                glu_buf[r0:r0 + CFM_HALO + CONV_ROWS, ls], cw_ref, ls) + cb_ref[:, ls]

    def h_dots(group):
        lo, width = W_IN_GROUPS[group]
        for j0 in range(0, d // SUB_BLOCK, SUBS_PER_DOT):
            cols = [(lo + i) * d + (j0 + jj) * SUB_BLOCK
                    for jj in range(SUBS_PER_DOT) for i in range(width)]
            rhs = jnp.concatenate([w_in_ref[:, c:c + SUB_BLOCK] for c in cols], axis=1)
            yield j0, (_dot(h_buf[...], rhs), width)

    def part(res_width, i, jj):
        res, width = res_width
        c0 = (jj * width + i) * SUB_BLOCK
        return res[:, c0:c0 + SUB_BLOCK]

    def sub_blocks(j0):
        for jj in range(SUBS_PER_DOT):
            yield jj, slice((j0 + jj) * SUB_BLOCK, (j0 + jj + 1) * SUB_BLOCK)

    h_buf[...] = _modulated_rms(x_ref[0], n1g_ref[...], mod_ref[0, 0:1, :],
                                mod_ref[0, 1:2, :]).astype(BF16)

    for j0, res in h_dots("glu"):
        for jj, cs in sub_blocks(j0):
            glu_buf[CFM_HALO:CFM_HALO + t, cs] = part(res, 0, jj) * _sigmoid(part(res, 1, jj))

    for j0, res in h_dots("gates"):
        run_conv_jobs(jobs_per_slot)
        for jj, cs in sub_blocks(j0):
            for i in range(N_BRANCH):
                gate_buf[i, :, cs] = _sigmoid(part(res, i, jj)).astype(BF16)

    for j0, res in h_dots("short"):
        run_conv_jobs(jobs_per_slot)
        for jj, cs in sub_blocks(j0):
            cx = part(res, 1, jj) * part(res, 2, jj)
            cx_buf[0:SHORT_HALO, :] = cx_halo[:, cs]
            cx_buf[SHORT_HALO:SHORT_HALO + t, :] = cx
            cx_halo[:, cs] = cx_buf[t:t + SHORT_HALO, :]
            win = cx_buf[...]
            conv = wsh_ref[SHORT_K - 1:SHORT_K, cs] * cx
            for lag in range(1, SHORT_K):
                k = SHORT_K - 1 - lag
                conv = conv + wsh_ref[k:k + 1, cs] * pltpu.roll(win, lag, axis=0)[SHORT_HALO:, :]
            ya_buf[:, cs] = (part(res, 0, jj) * conv).astype(BF16)

    for j0, res in h_dots("sgu"):
        run_conv_jobs(jobs_per_slot)
        for jj, cs in sub_blocks(j0):
            gu_buf[:, cs] = _gelu_tanh(part(res, 0, jj)).astype(BF16)
            ln_buf[:, cs] = _gelu_tanh(part(res, 1, jj))
    assert not conv_jobs

    mu, rstd = _row_stats(ln_buf[...], d)
    tri = (lax.broadcasted_iota(jnp.int32, (CHUNK, CHUNK), 0)
           >= lax.broadcasted_iota(jnp.int32, (CHUNK, CHUNK), 1))
    gd = d // N_SGU_GROUPS
    chunks = [slice(n * CHUNK, (n + 1) * CHUNK) for n in range(t // CHUNK)]
    for g in range(N_SGU_GROUPS):
        gs = slice(g * gd, (g + 1) * gd)
        vn = ((ln_buf[:, gs] - mu) * rstd * lng_ref[:, gs] + lnb_ref[:, gs]).astype(BF16)
        wg = jnp.where(tri, wsgu_ref[g].astype(F32), 0.0).astype(BF16)
        mixed = _dot(wg, jnp.concatenate([vn[rs, :] for rs in chunks], axis=1))
        bg = bsgu_ref[:, g:g + 1]
        for n, rs in enumerate(chunks):
            yb_buf[rs, gs] = (gu_buf[rs, gs].astype(F32)
                              * (mixed[:, n * gd:(n + 1) * gd] + bg)).astype(BF16)

    for cs in col_blocks:
        m = gate_buf[0, :, cs].astype(F32) * _dot(ya_buf[...], w_a_ref[:, cs])
        m = m + gate_buf[1, :, cs].astype(F32) * _dot(yb_buf[...], w_b_ref[:, cs])
        ln_buf[:, cs] = m

    mu, rstd = _row_stats(ln_c_buf[...], d)
    yc_buf = ya_buf
    yc_buf[...] = _silu((ln_c_buf[...] - mu) * rstd * clg_ref[...] + clb_ref[...]).astype(BF16)
    mg_buf = gu_buf
    for cs in col_blocks:
        m = ln_buf[:, cs] + gate_buf[2, :, cs].astype(F32) * _dot(yc_buf[...], w_c_ref[:, cs])
        mg_buf[:, cs] = m.astype(BF16)
    for cs in col_blocks:
        o = _dot(mg_buf[...], w_o_ref[:, cs])
        out_ref[0, :, cs] = x_ref[0, :, cs] + mod_ref[0, 2:3, cs] * o

    glu_buf[0:CFM_HALO, :] = glu_buf[t:t + CFM_HALO, :]


def _resident(shape):
    zeros = (0,) * len(shape)
    return pl.BlockSpec(shape, lambda b, s: zeros, pipeline_mode=pl.Buffered(1))


def _mixer(x, mod, n1g, w_in, wsh, w_a, lng, lnb, wsgu, bsgu_t, w_b, cw, cb, clg, clb, w_c, w_o):
    bn, sq, d = x.shape
    t = TOK_TILE
    params = (n1g, w_in, wsh, w_a, lng, lnb, wsgu, bsgu_t, w_b, cw, cb, clg, clb, w_c, w_o)
    return pl.pallas_call(
        _mixer_kernel,
        grid=(bn, sq // t),
        in_specs=[pl.BlockSpec((1, t, d), lambda b, s: (b, s, 0)),
                  pl.BlockSpec((1, N_MOD, d), lambda b, s: (b, 0, 0))]
                 + [_resident(p.shape) for p in params],
        out_specs=pl.BlockSpec((1, t, d), lambda b, s: (b, s, 0)),
        out_shape=jax.ShapeDtypeStruct(x.shape, F32),
        scratch_shapes=[
            pltpu.VMEM((t, d), BF16),
            pltpu.VMEM((SHORT_HALO + t, SUB_BLOCK), F32),
            pltpu.VMEM((SHORT_HALO, d), F32),
            pltpu.VMEM((CFM_HALO + t, d), F32),
            pltpu.VMEM((t, d), BF16),
            pltpu.VMEM((t, d), BF16),
            pltpu.VMEM((t, d), BF16),
            pltpu.VMEM((t, d), F32),
            pltpu.VMEM((t, d), F32),
            pltpu.VMEM((N_BRANCH, t, d), BF16),
        ],
        compiler_params=pltpu.CompilerParams(
            dimension_semantics=("arbitrary", "arbitrary"),
            vmem_limit_bytes=VMEM_LIMIT_BYTES),
        name="token_mixer",
    )(x, mod, *params)


def _ffn_kernel(x_ref, mod_ref, n2g_ref, w_in_hbm, w_out_hbm, fg_ref, out_ref, h_buf, a_buf,
                w_in_ref, w_out_ref, stage_in, stage_out, w_sem, *, layer, final_norm):
    t, d = h_buf.shape
    d_ff = a_buf.shape[1]

    @pl.when(_is_first_step())
    def _():
        for hbm, vmem, stage in ((w_in_hbm, w_in_ref, stage_in), (w_out_hbm, w_out_ref, stage_out)):
            rows = stage.shape[1]
            jobs = [(hbm.at[layer, pl.ds(r0, rows), :], vmem.at[pl.ds(r0, rows), :])
                    for r0 in range(0, vmem.shape[0], rows)]
            _load_weights_bf16(jobs, (stage.at[0], stage.at[1]), w_sem)

    h_buf[...] = _modulated_rms(x_ref[0], n2g_ref[...], mod_ref[0, 3:4, :],
                                mod_ref[0, 4:5, :]).astype(BF16)
    for j in range(0, d_ff, COL_BLOCK):
        cs = slice(j, min(j + COL_BLOCK, d_ff))
        us = slice(d_ff + cs.start, d_ff + cs.stop)
        hb = h_buf[...]
        a_buf[:, cs] = (_silu(_dot(hb, w_in_ref[:, cs])) * _dot(hb, w_in_ref[:, us])).astype(BF16)
    for j in range(0, d, COL_BLOCK):
        cs = slice(j, j + COL_BLOCK)
        o = _dot(a_buf[...], w_out_ref[:, cs])
        out_ref[0, :, cs] = x_ref[0, :, cs] + mod_ref[0, 5:6, cs] * o
    if final_norm:
        y = out_ref[0]
        ms = jnp.mean(y * y, axis=-1, keepdims=True)
        out_ref[0] = y * lax.rsqrt(ms + EPS) * fg_ref[...]


def _ffn(x, mod, n2g, w_in, w_out, fg, *, layer, final_norm):
    bn, sq, d = x.shape
    d_ff = w_out.shape[1]
    t = TOK_TILE
    assert w_in.shape[1] % FFN_LOAD_CHUNKS == 0 and w_out.shape[1] % (16 * FFN_LOAD_CHUNKS) == 0
    hbm = pl.BlockSpec(memory_space=pl.ANY)
    return pl.pallas_call(
        functools.partial(_ffn_kernel, layer=layer, final_norm=final_norm),
        grid=(bn, sq // t),
        in_specs=[pl.BlockSpec((1, t, d), lambda b, s: (b, s, 0)),
                  pl.BlockSpec((1, N_MOD, d), lambda b, s: (b, 0, 0)),
                  _resident(n2g.shape), hbm, hbm, _resident(fg.shape)],
        out_specs=pl.BlockSpec((1, t, d), lambda b, s: (b, s, 0)),
        out_shape=jax.ShapeDtypeStruct(x.shape, F32),
        scratch_shapes=[
            pltpu.VMEM((t, d), BF16),
            pltpu.VMEM((t, d_ff), BF16),
            pltpu.VMEM(w_in.shape[1:], BF16),
            pltpu.VMEM(w_out.shape[1:], BF16),
            pltpu.VMEM((2, w_in.shape[1] // FFN_LOAD_CHUNKS, w_in.shape[2]), F32),
            pltpu.VMEM((2, w_out.shape[1] // FFN_LOAD_CHUNKS, w_out.shape[2]), F32),
            pltpu.SemaphoreType.DMA((2,)),
        ],
        compiler_params=pltpu.CompilerParams(
            dimension_semantics=("arbitrary", "arbitrary"),
            vmem_limit_bytes=VMEM_LIMIT_BYTES),
        name="swiglu_final" if final_norm else "swiglu",
    )(x, mod, n2g, w_in, w_out, fg)


def kernel(x, c, w_ada, b_ada, norm1_g, w_in, w_short, w_a_out, sgu_ln_g, sgu_ln_b, w_sgu, b_sgu,
           w_b_out, cfm_conv_w, cfm_conv_b, cfm_ln_g, cfm_ln_b, w_c_out, w_o, norm2_g, w_ffn_in,
           w_ffn_out, final_g):
    depth = w_in.shape[0]
    d = x.shape[-1]
    assert x.shape[1] % TOK_TILE == 0 and TOK_TILE % CHUNK == 0 and d % COL_BLOCK == 0
    assert TOK_TILE % CONV_ROWS == 0 and w_sgu.shape[-1] == CHUNK
    assert w_short.shape[1] == SHORT_K and cfm_conv_w.shape[1] == CFM_K

    mod = _modulation(c, w_ada, b_ada)
    row = lambda p: p.reshape(1, -1)
    fg = row(final_g)
    for l in range(depth):
        x = _mixer(x, mod[l], row(norm1_g[l]), w_in[l].astype(BF16), w_short[l],
                   w_a_out[l].astype(BF16), row(sgu_ln_g[l]), row(sgu_ln_b[l]), w_sgu[l].astype(BF16),
                   b_sgu[l].T, w_b_out[l].astype(BF16), cfm_conv_w[l], row(cfm_conv_b[l]),
                   row(cfm_ln_g[l]), row(cfm_ln_b[l]), w_c_out[l].astype(BF16),
                   w_o[l].astype(BF16))
        x = _ffn(x, mod[l], row(norm2_g[l]), w_ffn_in, w_ffn_out, fg, layer=l,
                 final_norm=(l == depth - 1))
    return x
```

```python
import functools

import jax
import jax.numpy as jnp
from jax import lax
from jax.experimental import pallas as pl
from jax.experimental.pallas import tpu as pltpu

EPS = 1e-6
CHUNK = 128
N_SGU_GROUPS = 8
SHORT_K = 3
CFM_K = 31
N_MOD = 6
N_BRANCH = 3

TOK_TILE = 512
COL_BLOCK = 512
CONV_ROWS = 64
SUB_BLOCK = 256
SUBS_PER_DOT = 2
W_IN_GROUPS = {"short": (0, 3), "sgu": (3, 2), "glu": (5, 2), "gates": (7, 3)}
FFN_LOAD_CHUNKS = 8
FFN_SUBS_PER_DOT = 4
SUBLANES = 8
LANES = 128
SHORT_HALO = 8
CFM_HALO = 32
VMEM_LIMIT_BYTES = 63 * 1024 * 1024

F32 = jnp.float32
BF16 = jnp.bfloat16


def _dot(a, b):
    return jnp.dot(a, b, preferred_element_type=F32)


def _sigmoid(x):
    return 1.0 / (1.0 + jnp.exp(-x))


def _silu(x):
    return x * _sigmoid(x)


def _gelu_tanh(x):
    c = 0.7978845608028654
    return 0.5 * x * (1.0 + jnp.tanh(c * (x + 0.044715 * (x * x * x))))


def _row_stats(v, n):
    mu = jnp.sum(v, axis=-1, keepdims=True) * (1.0 / n)
    ex2 = jnp.sum(v * v, axis=-1, keepdims=True) * (1.0 / n)
    return mu, lax.rsqrt(ex2 - mu * mu + EPS)


def _modulated_rms(x, g, shift, scale):
    ms = jnp.mean(x * x, axis=-1, keepdims=True)
    return (x * lax.rsqrt(ms + EPS) * g) * (1.0 + scale) + shift


def _causal_conv_rows(win, w_ref, ls):
    k_taps = w_ref.shape[0]
    rows = win.shape[0] - CFM_HALO
    out = None
    for res in range(SUBLANES):
        lead = 0 if res == 0 else SUBLANES
        part = None
        for q in range((k_taps - 1 - res) // SUBLANES + 1):
            lag = SUBLANES * q + res
            lo = CFM_HALO - lead - SUBLANES * q
            term = w_ref[k_taps - 1 - lag:k_taps - lag, ls] * win[lo:lo + lead + rows, :]
            part = term if part is None else part + term
        if res == 0:
            out = part
        else:
            out = out + pltpu.roll(part, res, axis=0)[lead:, :]
    return out


def _mod_kernel(c_ref, w_ref, b_ref, o_ref):
    c = c_ref[...]
    o_ref[0] = _dot(_silu(c), w_ref[0]) + b_ref[0]


def _modulation(c, w_ada, b_ada):
    depth, d, n = w_ada.shape
    bn = c.shape[0]
    nb = n // 4
    out = pl.pallas_call(
        _mod_kernel,
        grid=(depth, n // nb),
        in_specs=[
            pl.BlockSpec((bn, d), lambda l, j: (0, 0)),
            pl.BlockSpec((1, d, nb), lambda l, j: (l, 0, j)),
            pl.BlockSpec((1, 1, nb), lambda l, j: (l, 0, j)),
        ],
        out_specs=pl.BlockSpec((1, bn, nb), lambda l, j: (l, 0, j)),
        out_shape=jax.ShapeDtypeStruct((depth, bn, n), F32),
        name="adaln_modulation",
    )(c, w_ada, b_ada.reshape(depth, 1, n))
    return out.reshape(depth, bn, N_MOD, d)


def _is_first_step():
    return jnp.logical_and(pl.program_id(0) == 0, pl.program_id(1) == 0)


def _load_weights_bf16(jobs, stage, sem):
    def copy(k):
        return pltpu.make_async_copy(jobs[k][0], stage[k % 2], sem.at[k % 2])

    copy(0).start()
    for k, (_, dst) in enumerate(jobs):
        if k + 1 < len(jobs):
            copy(k + 1).start()
        copy(k).wait()
        dst[...] = stage[k % 2][...].astype(BF16)


def _mixer_kernel(x_ref, mod_ref, n1g_ref, w_in_ref, wsh_ref, w_a_ref, lng_ref, lnb_ref,
                  wsgu_ref, bsgu_ref, w_b_ref, cw_ref, cb_ref, clg_ref, clb_ref, w_c_ref,
                  w_o_ref, out_ref,
                  h_buf, cx_buf, cx_halo, glu_buf, ya_buf, yb_buf, gu_buf, ln_buf,
                  ln_c_buf, gate_buf):
    t, d = h_buf.shape
    s = pl.program_id(1)

    @pl.when(s == 0)
    def _():
        cx_halo[...] = jnp.zeros((SHORT_HALO, d), F32)
        glu_buf[0:CFM_HALO, :] = jnp.zeros((CFM_HALO, d), F32)

    col_blocks = [slice(j, j + COL_BLOCK) for j in range(0, d, COL_BLOCK)]

    conv_jobs = [(r0, l0) for l0 in range(0, d, LANES) for r0 in range(0, t, CONV_ROWS)]
    jobs_per_slot = -(-len(conv_jobs) // (3 * (d // SUB_BLOCK // SUBS_PER_DOT)))

    def run_conv_jobs(n):
        for _ in range(min(n, len(conv_jobs))):
            r0, l0 = conv_jobs.pop(0)
            ls = slice(l0, l0 + LANES)
            ln_c_buf[r0:r0 + CONV_ROWS, ls] = _causal_conv_rows(
                glu_buf[r0:r0 + CFM_HALO + CONV_ROWS, ls], cw_ref, ls) + cb_ref[:, ls]

    def h_dots(group):
        lo, width = W_IN_GROUPS[group]
        for j0 in range(0, d // SUB_BLOCK, SUBS_PER_DOT):
            cols = [(lo + i) * d + (j0 + jj) * SUB_BLOCK
                    for jj in range(SUBS_PER_DOT) for i in range(width)]
            rhs = jnp.concatenate([w_in_ref[:, c:c + SUB_BLOCK] for c in cols], axis=1)
            yield j0, (_dot(h_buf[...], rhs), width)

    def part(res_width, i, jj):
        res, width = res_width
        c0 = (jj * width + i) * SUB_BLOCK
        return res[:, c0:c0 + SUB_BLOCK]

    def sub_blocks(j0):
        for jj in range(SUBS_PER_DOT):
            yield jj, slice((j0 + jj) * SUB_BLOCK, (j0 + jj + 1) * SUB_BLOCK)

    h_buf[...] = _modulated_rms(x_ref[0], n1g_ref[...], mod_ref[0, 0:1, :],
                                mod_ref[0, 1:2, :]).astype(BF16)

    for j0, res in h_dots("glu"):
        for jj, cs in sub_blocks(j0):
            glu_buf[CFM_HALO:CFM_HALO + t, cs] = part(res, 0, jj) * _sigmoid(part(res, 1, jj))

    for j0, res in h_dots("gates"):
        run_conv_jobs(jobs_per_slot)
        for jj, cs in sub_blocks(j0):
            for i in range(N_BRANCH):
                gate_buf[i, :, cs] = _sigmoid(part(res, i, jj)).astype(BF16)

    for j0, res in h_dots("short"):
        run_conv_jobs(jobs_per_slot)
        for jj, cs in sub_blocks(j0):
            cx = part(res, 1, jj) * part(res, 2, jj)
            cx_buf[0:SHORT_HALO, :] = cx_halo[:, cs]
            cx_buf[SHORT_HALO:SHORT_HALO + t, :] = cx
            cx_halo[:, cs] = cx_buf[t:t + SHORT_HALO, :]
            win = cx_buf[...]
            conv = wsh_ref[SHORT_K - 1:SHORT_K, cs] * cx
            for lag in range(1, SHORT_K):
                k = SHORT_K - 1 - lag
                conv = conv + wsh_ref[k:k + 1, cs] * pltpu.roll(win, lag, axis=0)[SHORT_HALO:, :]
            ya_buf[:, cs] = (part(res, 0, jj) * conv).astype(BF16)

    for j0, res in h_dots("sgu"):
        run_conv_jobs(jobs_per_slot)
        for jj, cs in sub_blocks(j0):
            gu_buf[:, cs] = _gelu_tanh(part(res, 0, jj)).astype(BF16)
            ln_buf[:, cs] = _gelu_tanh(part(res, 1, jj))
    assert not conv_jobs

    mu, rstd = _row_stats(ln_buf[...], d)
    tri = (lax.broadcasted_iota(jnp.int32, (CHUNK, CHUNK), 0)
           >= lax.broadcasted_iota(jnp.int32, (CHUNK, CHUNK), 1))
    gd = d // N_SGU_GROUPS
    chunks = [slice(n * CHUNK, (n + 1) * CHUNK) for n in range(t // CHUNK)]
    for g in range(N_SGU_GROUPS):
        gs = slice(g * gd, (g + 1) * gd)
        vn = ((ln_buf[:, gs] - mu) * rstd * lng_ref[:, gs] + lnb_ref[:, gs]).astype(BF16)
        wg = jnp.where(tri, wsgu_ref[g].astype(F32), 0.0).astype(BF16)
        mixed = _dot(wg, jnp.concatenate([vn[rs, :] for rs in chunks], axis=1))
        bg = bsgu_ref[:, g:g + 1]
        for n, rs in enumerate(chunks):
            yb_buf[rs, gs] = (gu_buf[rs, gs].astype(F32)
                              * (mixed[:, n * gd:(n + 1) * gd] + bg)).astype(BF16)

    for cs in col_blocks:
        m = gate_buf[0, :, cs].astype(F32) * _dot(ya_buf[...], w_a_ref[:, cs])
        m = m + gate_buf[1, :, cs].astype(F32) * _dot(yb_buf[...], w_b_ref[:, cs])
        ln_buf[:, cs] = m

    mu, rstd = _row_stats(ln_c_buf[...], d)
    yc_buf = ya_buf
    yc_buf[...] = _silu((ln_c_buf[...] - mu) * rstd * clg_ref[...] + clb_ref[...]).astype(BF16)
    mg_buf = gu_buf
    for cs in col_blocks:
        m = ln_buf[:, cs] + gate_buf[2, :, cs].astype(F32) * _dot(yc_buf[...], w_c_ref[:, cs])
        mg_buf[:, cs] = m.astype(BF16)
    for cs in col_blocks:
        o = _dot(mg_buf[...], w_o_ref[:, cs])
        out_ref[0, :, cs] = x_ref[0, :, cs] + mod_ref[0, 2:3, cs] * o

    glu_buf[0:CFM_HALO, :] = glu_buf[t:t + CFM_HALO, :]


def _resident(shape):
    zeros = (0,) * len(shape)
    return pl.BlockSpec(shape, lambda b, s: zeros, pipeline_mode=pl.Buffered(1))


def _mixer(x, mod, n1g, w_in, wsh, w_a, lng, lnb, wsgu, bsgu_t, w_b, cw, cb, clg, clb, w_c, w_o):
    bn, sq, d = x.shape
    t = TOK_TILE
    params = (n1g, w_in, wsh, w_a, lng, lnb, wsgu, bsgu_t, w_b, cw, cb, clg, clb, w_c, w_o)
    return pl.pallas_call(
        _mixer_kernel,
        grid=(bn, sq // t),
        in_specs=[pl.BlockSpec((1, t, d), lambda b, s: (b, s, 0)),
                  pl.BlockSpec((1, N_MOD, d), lambda b, s: (b, 0, 0))]
                 + [_resident(p.shape) for p in params],
        out_specs=pl.BlockSpec((1, t, d), lambda b, s: (b, s, 0)),
        out_shape=jax.ShapeDtypeStruct(x.shape, F32),
        scratch_shapes=[
            pltpu.VMEM((t, d), BF16),
            pltpu.VMEM((SHORT_HALO + t, SUB_BLOCK), F32),
            pltpu.VMEM((SHORT_HALO, d), F32),
            pltpu.VMEM((CFM_HALO + t, d), F32),
            pltpu.VMEM((t, d), BF16),
            pltpu.VMEM((t, d), BF16),
            pltpu.VMEM((t, d), BF16),
            pltpu.VMEM((t, d), F32),
            pltpu.VMEM((t, d), F32),
            pltpu.VMEM((N_BRANCH, t, d), BF16),
        ],
        compiler_params=pltpu.CompilerParams(
            dimension_semantics=("arbitrary", "arbitrary"),
            vmem_limit_bytes=VMEM_LIMIT_BYTES),
        name="token_mixer",
    )(x, mod, *params)


def _ffn_kernel(x_ref, mod_ref, n2g_ref, w_in_hbm, w_out_hbm, fg_ref, out_ref, h_buf, a_buf,
                w_in_ref, w_out_ref, stage_in, stage_out, w_sem, *, layer, final_norm):
    t, d = h_buf.shape
    d_ff = a_buf.shape[1]

    @pl.when(_is_first_step())
    def _():
        for hbm, vmem, stage in ((w_in_hbm, w_in_ref, stage_in), (w_out_hbm, w_out_ref, stage_out)):
            rows = stage.shape[1]
            jobs = [(hbm.at[layer, pl.ds(r0, rows), :], vmem.at[pl.ds(r0, rows), :])
                    for r0 in range(0, vmem.shape[0], rows)]
            _load_weights_bf16(jobs, (stage.at[0], stage.at[1]), w_sem)

    h_buf[...] = _modulated_rms(x_ref[0], n2g_ref[...], mod_ref[0, 3:4, :],
                                mod_ref[0, 4:5, :]).astype(BF16)
    for j0 in range(0, d_ff, FFN_SUBS_PER_DOT * SUB_BLOCK):
        subs = range(j0, min(j0 + FFN_SUBS_PER_DOT * SUB_BLOCK, d_ff), SUB_BLOCK)
        rhs = jnp.concatenate([w_in_ref[:, c:c + SUB_BLOCK] for j in subs for c in (j, d_ff + j)],
                              axis=1)
        res = _dot(h_buf[...], rhs)
        for k, j in enumerate(subs):
            gate = res[:, 2 * k * SUB_BLOCK:(2 * k + 1) * SUB_BLOCK]
            up = res[:, (2 * k + 1) * SUB_BLOCK:(2 * k + 2) * SUB_BLOCK]
            a_buf[:, j:j + SUB_BLOCK] = (_silu(gate) * up).astype(BF16)
    for j in range(0, d, COL_BLOCK):
        cs = slice(j, j + COL_BLOCK)
        o = _dot(a_buf[...], w_out_ref[:, cs])
        out_ref[0, :, cs] = x_ref[0, :, cs] + mod_ref[0, 5:6, cs] * o
    if final_norm:
        y = out_ref[0]
        ms = jnp.mean(y * y, axis=-1, keepdims=True)
        out_ref[0] = y * lax.rsqrt(ms + EPS) * fg_ref[...]


def _ffn(x, mod, n2g, w_in, w_out, fg, *, layer, final_norm):
    bn, sq, d = x.shape
    d_ff = w_out.shape[1]
    t = TOK_TILE
    assert w_in.shape[1] % FFN_LOAD_CHUNKS == 0 and w_out.shape[1] % (16 * FFN_LOAD_CHUNKS) == 0
    hbm = pl.BlockSpec(memory_space=pl.ANY)
    return pl.pallas_call(
        functools.partial(_ffn_kernel, layer=layer, final_norm=final_norm),
        grid=(bn, sq // t),
        in_specs=[pl.BlockSpec((1, t, d), lambda b, s: (b, s, 0)),
                  pl.BlockSpec((1, N_MOD, d), lambda b, s: (b, 0, 0)),
                  _resident(n2g.shape), hbm, hbm, _resident(fg.shape)],
        out_specs=pl.BlockSpec((1, t, d), lambda b, s: (b, s, 0)),
        out_shape=jax.ShapeDtypeStruct(x.shape, F32),
        scratch_shapes=[
            pltpu.VMEM((t, d), BF16),
            pltpu.VMEM((t, d_ff), BF16),
            pltpu.VMEM(w_in.shape[1:], BF16),
            pltpu.VMEM(w_out.shape[1:], BF16),
            pltpu.VMEM((2, w_in.shape[1] // FFN_LOAD_CHUNKS, w_in.shape[2]), F32),
            pltpu.VMEM((2, w_out.shape[1] // FFN_LOAD_CHUNKS, w_out.shape[2]), F32),
            pltpu.SemaphoreType.DMA((2,)),
        ],
        compiler_params=pltpu.CompilerParams(
            dimension_semantics=("arbitrary", "arbitrary"),
            vmem_limit_bytes=VMEM_LIMIT_BYTES),
        name="swiglu_final" if final_norm else "swiglu",
    )(x, mod, n2g, w_in, w_out, fg)


def kernel(x, c, w_ada, b_ada, norm1_g, w_in, w_short, w_a_out, sgu_ln_g, sgu_ln_b, w_sgu, b_sgu,
           w_b_out, cfm_conv_w, cfm_conv_b, cfm_ln_g, cfm_ln_b, w_c_out, w_o, norm2_g, w_ffn_in,
           w_ffn_out, final_g):
    depth = w_in.shape[0]
    d = x.shape[-1]
    assert x.shape[1] % TOK_TILE == 0 and TOK_TILE % CHUNK == 0 and d % COL_BLOCK == 0
    assert TOK_TILE % CONV_ROWS == 0 and w_sgu.shape[-1] == CHUNK
    assert w_short.shape[1] == SHORT_K and cfm_conv_w.shape[1] == CFM_K

    mod = _modulation(c, w_ada, b_ada)
    row = lambda p: p.reshape(1, -1)
    fg = row(final_g)
    for l in range(depth):
        x = _mixer(x, mod[l], row(norm1_g[l]), w_in[l].astype(BF16), w_short[l],
                   w_a_out[l].astype(BF16), row(sgu_ln_g[l]), row(sgu_ln_b[l]), w_sgu[l].astype(BF16),
                   b_sgu[l].T, w_b_out[l].astype(BF16), cfm_conv_w[l], row(cfm_conv_b[l]),
                   row(cfm_ln_g[l]), row(cfm_ln_b[l]), w_c_out[l].astype(BF16),
                   w_o[l].astype(BF16))
        x = _ffn(x, mod[l], row(norm2_g[l]), w_ffn_in, w_ffn_out, fg, layer=l,
                 final_norm=(l == depth - 1))
    return x
```

```python
import functools

import jax
import jax.numpy as jnp
from jax import lax
from jax.experimental import pallas as pl
from jax.experimental.pallas import tpu as pltpu

EPS = 1e-6
CHUNK = 128
N_SGU_GROUPS = 8
SHORT_K = 3
CFM_K = 31
N_MOD = 6
N_BRANCH = 3

TOK_TILE = 512
COL_BLOCK = 512
CONV_ROWS = 64
SUB_BLOCK = 256
SUBS_PER_DOT = 2
W_IN_GROUPS = {"short": (0, 3), "sgu": (3, 2), "glu": (5, 2), "gates": (7, 3)}
CAST_ROW_BLOCK = 256
CAST_COL_BLOCK = 5120
FFN_LOAD_CHUNKS = 8
FFN_SUBS_PER_DOT = 4
SUBLANES = 8
LANES = 128
SHORT_HALO = 8
CFM_HALO = 32
VMEM_LIMIT_BYTES = 63 * 1024 * 1024

F32 = jnp.float32
BF16 = jnp.bfloat16


def _dot(a, b):
    return jnp.dot(a, b, preferred_element_type=F32)


def _sigmoid(x):
    return 1.0 / (1.0 + jnp.exp(-x))


def _silu(x):
    return x * _sigmoid(x)


def _gelu_tanh(x):
    c = 0.7978845608028654
    return 0.5 * x * (1.0 + jnp.tanh(c * (x + 0.044715 * (x * x * x))))


def _row_stats(v, n):
    mu = jnp.sum(v, axis=-1, keepdims=True) * (1.0 / n)
    ex2 = jnp.sum(v * v, axis=-1, keepdims=True) * (1.0 / n)
    return mu, lax.rsqrt(ex2 - mu * mu + EPS)


def _modulated_rms(x, g, shift, scale):
    ms = jnp.mean(x * x, axis=-1, keepdims=True)
    return (x * lax.rsqrt(ms + EPS) * g) * (1.0 + scale) + shift


def _causal_conv_rows(win, w_ref, ls):
    k_taps = w_ref.shape[0]
    rows = win.shape[0] - CFM_HALO
    out = None
    for res in range(SUBLANES):
        lead = 0 if res == 0 else SUBLANES
        part = None
        for q in range((k_taps - 1 - res) // SUBLANES + 1):
            lag = SUBLANES * q + res
            lo = CFM_HALO - lead - SUBLANES * q
            term = w_ref[k_taps - 1 - lag:k_taps - lag, ls] * win[lo:lo + lead + rows, :]
            part = term if part is None else part + term
        if res == 0:
            out = part
        else:
            out = out + pltpu.roll(part, res, axis=0)[lead:, :]
    return out


def _mod_kernel(c_ref, w_ref, b_ref, o_ref):
    c = c_ref[...]
    o_ref[0] = _dot(_silu(c), w_ref[0]) + b_ref[0]


def _modulation(c, w_ada, b_ada):
    depth, d, n = w_ada.shape
    bn = c.shape[0]
    nb = n // 4
    out = pl.pallas_call(
        _mod_kernel,
        grid=(depth, n // nb),
        in_specs=[
            pl.BlockSpec((bn, d), lambda l, j: (0, 0)),
            pl.BlockSpec((1, d, nb), lambda l, j: (l, 0, j)),
            pl.BlockSpec((1, 1, nb), lambda l, j: (l, 0, j)),
        ],
        out_specs=pl.BlockSpec((1, bn, nb), lambda l, j: (l, 0, j)),
        out_shape=jax.ShapeDtypeStruct((depth, bn, n), F32),
        name="adaln_modulation",
    )(c, w_ada, b_ada.reshape(depth, 1, n))
    return out.reshape(depth, bn, N_MOD, d)


def _is_first_step():
    return jnp.logical_and(pl.program_id(0) == 0, pl.program_id(1) == 0)


def _load_weights_bf16(jobs, stage, sem):
    def copy(k):
        return pltpu.make_async_copy(jobs[k][0], stage[k % 2], sem.at[k % 2])

    copy(0).start()
    for k, (_, dst) in enumerate(jobs):
        if k + 1 < len(jobs):
            copy(k + 1).start()
        copy(k).wait()
        dst[...] = stage[k % 2][...].astype(BF16)


def _mixer_kernel(x_ref, mod_ref, n1g_ref, w_in_ref, wsh_ref, w_a_ref, lng_ref, lnb_ref,
                  wsgu_ref, bsgu_ref, w_b_ref, cw_ref, cb_ref, clg_ref, clb_ref, w_c_ref,
                  w_o_ref, out_ref,
                  h_buf, cx_buf, cx_halo, glu_buf, ya_buf, yb_buf, gu_buf, ln_buf,
                  ln_c_buf, gate_buf):
    t, d = h_buf.shape
    s = pl.program_id(1)

    @pl.when(s == 0)
    def _():
        cx_halo[...] = jnp.zeros((SHORT_HALO, d), F32)
        glu_buf[0:CFM_HALO, :] = jnp.zeros((CFM_HALO, d), F32)

    col_blocks = [slice(j, j + COL_BLOCK) for j in range(0, d, COL_BLOCK)]

    conv_jobs = [(r0, l0) for l0 in range(0, d, LANES) for r0 in range(0, t, CONV_ROWS)]
    jobs_per_slot = -(-len(conv_jobs) // (3 * (d // SUB_BLOCK // SUBS_PER_DOT)))

    def run_conv_jobs(n):
        for _ in range(min(n, len(conv_jobs))):
            r0, l0 = conv_jobs.pop(0)
            ls = slice(l0, l0 + LANES)
            ln_c_buf[r0:r0 + CONV_ROWS, ls] = _causal_conv_rows(
                glu_buf[r0:r0 + CFM_HALO + CONV_ROWS, ls], cw_ref, ls) + cb_ref[:, ls]

    def h_dots(group):
        lo, width = W_IN_GROUPS[group]
        for j0 in range(0, d // SUB_BLOCK, SUBS_PER_DOT):
            cols = [(lo + i) * d + (j0 + jj) * SUB_BLOCK
                    for jj in range(SUBS_PER_DOT) for i in range(width)]
            rhs = jnp.concatenate([w_in_ref[:, c:c + SUB_BLOCK] for c in cols], axis=1)
            yield j0, (_dot(h_buf[...], rhs), width)

    def part(res_width, i, jj):
        res, width = res_width
        c0 = (jj * width + i) * SUB_BLOCK
        return res[:, c0:c0 + SUB_BLOCK]

    def sub_blocks(j0):
        for jj in range(SUBS_PER_DOT):
            yield jj, slice((j0 + jj) * SUB_BLOCK, (j0 + jj + 1) * SUB_BLOCK)

    h_buf[...] = _modulated_rms(x_ref[0], n1g_ref[...], mod_ref[0, 0:1, :],
                                mod_ref[0, 1:2, :]).astype(BF16)

    for j0, res in h_dots("glu"):
        for jj, cs in sub_blocks(j0):
            glu_buf[CFM_HALO:CFM_HALO + t, cs] = part(res, 0, jj) * _sigmoid(part(res, 1, jj))

    for j0, res in h_dots("gates"):
        run_conv_jobs(jobs_per_slot)
        for jj, cs in sub_blocks(j0):
            for i in range(N_BRANCH):
                gate_buf[i, :, cs] = _sigmoid(part(res, i, jj)).astype(BF16)

    for j0, res in h_dots("short"):
        run_conv_jobs(jobs_per_slot)
        for jj, cs in sub_blocks(j0):
            cx = part(res, 1, jj) * part(res, 2, jj)
            cx_buf[0:SHORT_HALO, :] = cx_halo[:, cs]
            cx_buf[SHORT_HALO:SHORT_HALO + t, :] = cx
            cx_halo[:, cs] = cx_buf[t:t + SHORT_HALO, :]
            win = cx_buf[...]
            conv = wsh_ref[SHORT_K - 1:SHORT_K, cs] * cx
            for lag in range(1, SHORT_K):
                k = SHORT_K - 1 - lag
                conv = conv + wsh_ref[k:k + 1, cs] * pltpu.roll(win, lag, axis=0)[SHORT_HALO:, :]
            ya_buf[:, cs] = (part(res, 0, jj) * conv).astype(BF16)

    for j0, res in h_dots("sgu"):
        run_conv_jobs(jobs_per_slot)
        for jj, cs in sub_blocks(j0):
            gu_buf[:, cs] = _gelu_tanh(part(res, 0, jj)).astype(BF16)
            ln_buf[:, cs] = _gelu_tanh(part(res, 1, jj))
    assert not conv_jobs

    mu, rstd = _row_stats(ln_buf[...], d)
    tri = (lax.broadcasted_iota(jnp.int32, (CHUNK, CHUNK), 0)
           >= lax.broadcasted_iota(jnp.int32, (CHUNK, CHUNK), 1))
    gd = d // N_SGU_GROUPS
    chunks = [slice(n * CHUNK, (n + 1) * CHUNK) for n in range(t // CHUNK)]
    for g in range(N_SGU_GROUPS):
        gs = slice(g * gd, (g + 1) * gd)
        vn = ((ln_buf[:, gs] - mu) * rstd * lng_ref[:, gs] + lnb_ref[:, gs]).astype(BF16)
        wg = jnp.where(tri, wsgu_ref[g].astype(F32), 0.0).astype(BF16)
        mixed = _dot(wg, jnp.concatenate([vn[rs, :] for rs in chunks], axis=1))
        bg = bsgu_ref[:, g:g + 1]
        for n, rs in enumerate(chunks):
            yb_buf[rs, gs] = (gu_buf[rs, gs].astype(F32)
                              * (mixed[:, n * gd:(n + 1) * gd] + bg)).astype(BF16)

    for cs in col_blocks:
        m = gate_buf[0, :, cs].astype(F32) * _dot(ya_buf[...], w_a_ref[:, cs])
        m = m + gate_buf[1, :, cs].astype(F32) * _dot(yb_buf[...], w_b_ref[:, cs])
        ln_buf[:, cs] = m

    mu, rstd = _row_stats(ln_c_buf[...], d)
    yc_buf = ya_buf
    yc_buf[...] = _silu((ln_c_buf[...] - mu) * rstd * clg_ref[...] + clb_ref[...]).astype(BF16)
    mg_buf = gu_buf
    for cs in col_blocks:
        m = ln_buf[:, cs] + gate_buf[2, :, cs].astype(F32) * _dot(yc_buf[...], w_c_ref[:, cs])
        mg_buf[:, cs] = m.astype(BF16)
    for cs in col_blocks:
        o = _dot(mg_buf[...], w_o_ref[:, cs])
        out_ref[0, :, cs] = x_ref[0, :, cs] + mod_ref[0, 2:3, cs] * o

    glu_buf[0:CFM_HALO, :] = glu_buf[t:t + CFM_HALO, :]


def _cast_kernel(*refs):
    n = len(refs) // 2
    for src, dst in zip(refs[:n], refs[n:]):
        dst[...] = src[0].astype(BF16)


def _layer_to_bf16(ws, layer, row_block):
    _, rows, cols = ws[0].shape
    col_block = min(cols, CAST_COL_BLOCK)
    return pl.pallas_call(
        _cast_kernel,
        grid=(rows // row_block, cols // col_block),
        in_specs=[pl.BlockSpec((1, row_block, col_block), lambda i, j: (layer, i, j)) for _ in ws],
        out_specs=[pl.BlockSpec((row_block, col_block), lambda i, j: (i, j)) for _ in ws],
        out_shape=[jax.ShapeDtypeStruct((rows, cols), BF16) for _ in ws],
        name="weights_to_bf16",
    )(*ws)


def _resident(shape):
    zeros = (0,) * len(shape)
    return pl.BlockSpec(shape, lambda b, s: zeros, pipeline_mode=pl.Buffered(1))


def _mixer(x, mod, n1g, w_in, wsh, w_a, lng, lnb, wsgu, bsgu_t, w_b, cw, cb, clg, clb, w_c, w_o):
    bn, sq, d = x.shape
    t = TOK_TILE
    params = (n1g, w_in, wsh, w_a, lng, lnb, wsgu, bsgu_t, w_b, cw, cb, clg, clb, w_c, w_o)
    return pl.pallas_call(
        _mixer_kernel,
        grid=(bn, sq // t),
        in_specs=[pl.BlockSpec((1, t, d), lambda b, s: (b, s, 0)),
                  pl.BlockSpec((1, N_MOD, d), lambda b, s: (b, 0, 0))]
                 + [_resident(p.shape) for p in params],
        out_specs=pl.BlockSpec((1, t, d), lambda b, s: (b, s, 0)),
        out_shape=jax.ShapeDtypeStruct(x.shape, F32),
        scratch_shapes=[
            pltpu.VMEM((t, d), BF16),
            pltpu.VMEM((SHORT_HALO + t, SUB_BLOCK), F32),
            pltpu.VMEM((SHORT_HALO, d), F32),
            pltpu.VMEM((CFM_HALO + t, d), F32),
            pltpu.VMEM((t, d), BF16),
            pltpu.VMEM((t, d), BF16),
            pltpu.VMEM((t, d), BF16),
            pltpu.VMEM((t, d), F32),
            pltpu.VMEM((t, d), F32),
            pltpu.VMEM((N_BRANCH, t, d), BF16),
        ],
        compiler_params=pltpu.CompilerParams(
            dimension_semantics=("arbitrary", "arbitrary"),
            vmem_limit_bytes=VMEM_LIMIT_BYTES),
        name="token_mixer",
    )(x, mod, *params)


def _ffn_kernel(x_ref, mod_ref, n2g_ref, w_in_hbm, w_out_hbm, fg_ref, out_ref, h_buf, a_buf,
                w_in_ref, w_out_ref, stage_in, stage_out, w_sem, *, layer, final_norm):
    t, d = h_buf.shape
    d_ff = a_buf.shape[1]

    @pl.when(_is_first_step())
    def _():
        for hbm, vmem, stage in ((w_in_hbm, w_in_ref, stage_in), (w_out_hbm, w_out_ref, stage_out)):
            rows = stage.shape[1]
            jobs = [(hbm.at[layer, pl.ds(r0, rows), :], vmem.at[pl.ds(r0, rows), :])
                    for r0 in range(0, vmem.shape[0], rows)]
            _load_weights_bf16(jobs, (stage.at[0], stage.at[1]), w_sem)

    h_buf[...] = _modulated_rms(x_ref[0], n2g_ref[...], mod_ref[0, 3:4, :],
                                mod_ref[0, 4:5, :]).astype(BF16)
    for j0 in range(0, d_ff, FFN_SUBS_PER_DOT * SUB_BLOCK):
        subs = range(j0, min(j0 + FFN_SUBS_PER_DOT * SUB_BLOCK, d_ff), SUB_BLOCK)
        rhs = jnp.concatenate([w_in_ref[:, c:c + SUB_BLOCK] for j in subs for c in (j, d_ff + j)],
                              axis=1)
        res = _dot(h_buf[...], rhs)
        for k, j in enumerate(subs):
            gate = res[:, 2 * k * SUB_BLOCK:(2 * k + 1) * SUB_BLOCK]
            up = res[:, (2 * k + 1) * SUB_BLOCK:(2 * k + 2) * SUB_BLOCK]
            a_buf[:, j:j + SUB_BLOCK] = (_silu(gate) * up).astype(BF16)
    for j in range(0, d, COL_BLOCK):
        cs = slice(j, j + COL_BLOCK)
        o = _dot(a_buf[...], w_out_ref[:, cs])
        out_ref[0, :, cs] = x_ref[0, :, cs] + mod_ref[0, 5:6, cs] * o
    if final_norm:
        y = out_ref[0]
        ms = jnp.mean(y * y, axis=-1, keepdims=True)
        out_ref[0] = y * lax.rsqrt(ms + EPS) * fg_ref[...]


def _ffn(x, mod, n2g, w_in, w_out, fg, *, layer, final_norm):
    bn, sq, d = x.shape
    d_ff = w_out.shape[1]
    t = TOK_TILE
    assert w_in.shape[1] % FFN_LOAD_CHUNKS == 0 and w_out.shape[1] % (16 * FFN_LOAD_CHUNKS) == 0
    hbm = pl.BlockSpec(memory_space=pl.ANY)
    return pl.pallas_call(
        functools.partial(_ffn_kernel, layer=layer, final_norm=final_norm),
        grid=(bn, sq // t),
        in_specs=[pl.BlockSpec((1, t, d), lambda b, s: (b, s, 0)),
                  pl.BlockSpec((1, N_MOD, d), lambda b, s: (b, 0, 0)),
                  _resident(n2g.shape), hbm, hbm, _resident(fg.shape)],
        out_specs=pl.BlockSpec((1, t, d), lambda b, s: (b, s, 0)),
        out_shape=jax.ShapeDtypeStruct(x.shape, F32),
        scratch_shapes=[
            pltpu.VMEM((t, d), BF16),
            pltpu.VMEM((t, d_ff), BF16),
            pltpu.VMEM(w_in.shape[1:], BF16),
            pltpu.VMEM(w_out.shape[1:], BF16),
            pltpu.VMEM((2, w_in.shape[1] // FFN_LOAD_CHUNKS, w_in.shape[2]), F32),
            pltpu.VMEM((2, w_out.shape[1] // FFN_LOAD_CHUNKS, w_out.shape[2]), F32),
            pltpu.SemaphoreType.DMA((2,)),
        ],
        compiler_params=pltpu.CompilerParams(
            dimension_semantics=("arbitrary", "arbitrary"),
            vmem_limit_bytes=VMEM_LIMIT_BYTES),
        name="swiglu_final" if final_norm else "swiglu",
    )(x, mod, n2g, w_in, w_out, fg)


def kernel(x, c, w_ada, b_ada, norm1_g, w_in, w_short, w_a_out, sgu_ln_g, sgu_ln_b, w_sgu, b_sgu,
           w_b_out, cfm_conv_w, cfm_conv_b, cfm_ln_g, cfm_ln_b, w_c_out, w_o, norm2_g, w_ffn_in,
           w_ffn_out, final_g):
    depth = w_in.shape[0]
    d = x.shape[-1]
    assert x.shape[1] % TOK_TILE == 0 and TOK_TILE % CHUNK == 0 and d % COL_BLOCK == 0
    assert TOK_TILE % CONV_ROWS == 0 and w_sgu.shape[-1] == CHUNK
    assert w_short.shape[1] == SHORT_K and cfm_conv_w.shape[1] == CFM_K

    mod = _modulation(c, w_ada, b_ada)
    row = lambda p: p.reshape(1, -1)
    fg = row(final_g)
    for l in range(depth):
        w_in_l, = _layer_to_bf16((w_in,), l, CAST_ROW_BLOCK)
        w_a_l, w_b_l, w_c_l, w_o_l = _layer_to_bf16((w_a_out, w_b_out, w_c_out, w_o), l,
                                                    2 * CAST_ROW_BLOCK)
        x = _mixer(x, mod[l], row(norm1_g[l]), w_in_l, w_short[l], w_a_l, row(sgu_ln_g[l]),
                   row(sgu_ln_b[l]), w_sgu[l].astype(BF16), b_sgu[l].T, w_b_l, cfm_conv_w[l],
                   row(cfm_conv_b[l]), row(cfm_ln_g[l]), row(cfm_ln_b[l]), w_c_l, w_o_l)
        x = _ffn(x, mod[l], row(norm2_g[l]), w_ffn_in, w_ffn_out, fg, layer=l,
                 final_norm=(l == depth - 1))
    return x
```

```python
import functools

import jax
import jax.numpy as jnp
from jax import lax
from jax.experimental import pallas as pl
from jax.experimental.pallas import tpu as pltpu

EPS = 1e-6
CHUNK = 128
N_SGU_GROUPS = 8
SHORT_K = 3
CFM_K = 31
N_MOD = 6
N_BRANCH = 3

TOK_TILE = 512
FFN_TOK_TILE = 1024
COL_BLOCK = 512
CONV_ROWS = 64
SUB_BLOCK = 256
SUBS_PER_DOT = 2
W_IN_GROUPS = {"short": (0, 3), "sgu": (3, 2), "glu": (5, 2), "gates": (7, 3)}
CAST_ROW_BLOCK = 256
CAST_COL_BLOCK = 5120
FFN_LOAD_CHUNKS = 16
FFN_SUBS_PER_DOT = 4
SUBLANES = 8
LANES = 128
SHORT_HALO = 8
CFM_HALO = 32
VMEM_LIMIT_BYTES = 63 * 1024 * 1024

F32 = jnp.float32
BF16 = jnp.bfloat16


def _dot(a, b):
    return jnp.dot(a, b, preferred_element_type=F32)


def _sigmoid(x):
    return 1.0 / (1.0 + jnp.exp(-x))


def _silu(x):
    return x * _sigmoid(x)


def _gelu_tanh(x):
    c = 0.7978845608028654
    return 0.5 * x * (1.0 + jnp.tanh(c * (x + 0.044715 * (x * x * x))))


def _row_stats(v, n):
    mu = jnp.sum(v, axis=-1, keepdims=True) * (1.0 / n)
    ex2 = jnp.sum(v * v, axis=-1, keepdims=True) * (1.0 / n)
    return mu, lax.rsqrt(ex2 - mu * mu + EPS)


def _modulated_rms(x, g, shift, scale):
    ms = jnp.mean(x * x, axis=-1, keepdims=True)
    return (x * lax.rsqrt(ms + EPS) * g) * (1.0 + scale) + shift


def _causal_conv_rows(win, w_ref, ls):
    k_taps = w_ref.shape[0]
    rows = win.shape[0] - CFM_HALO
    out = None
    for res in range(SUBLANES):
        lead = 0 if res == 0 else SUBLANES
        part = None
        for q in range((k_taps - 1 - res) // SUBLANES + 1):
            lag = SUBLANES * q + res
            lo = CFM_HALO - lead - SUBLANES * q
            term = w_ref[k_taps - 1 - lag:k_taps - lag, ls] * win[lo:lo + lead + rows, :]
            part = term if part is None else part + term
        if res == 0:
            out = part
        else:
            out = out + pltpu.roll(part, res, axis=0)[lead:, :]
    return out


def _mod_kernel(c_ref, w_ref, b_ref, o_ref):
    c = c_ref[...]
    o_ref[0] = _dot(_silu(c), w_ref[0]) + b_ref[0]


def _modulation(c, w_ada, b_ada):
    depth, d, n = w_ada.shape
    bn = c.shape[0]
    nb = n // 4
    out = pl.pallas_call(
        _mod_kernel,
        grid=(depth, n // nb),
        in_specs=[
            pl.BlockSpec((bn, d), lambda l, j: (0, 0)),
            pl.BlockSpec((1, d, nb), lambda l, j: (l, 0, j)),
            pl.BlockSpec((1, 1, nb), lambda l, j: (l, 0, j)),
        ],
        out_specs=pl.BlockSpec((1, bn, nb), lambda l, j: (l, 0, j)),
        out_shape=jax.ShapeDtypeStruct((depth, bn, n), F32),
        name="adaln_modulation",
    )(c, w_ada, b_ada.reshape(depth, 1, n))
    return out.reshape(depth, bn, N_MOD, d)


def _is_first_step():
    return jnp.logical_and(pl.program_id(0) == 0, pl.program_id(1) == 0)


def _load_weights_bf16(jobs, stage, sem):
    def copy(k):
        return pltpu.make_async_copy(jobs[k][0], stage[k % 2], sem.at[k % 2])

    copy(0).start()
    for k, (_, dst) in enumerate(jobs):
        if k + 1 < len(jobs):
            copy(k + 1).start()
        copy(k).wait()
        dst[...] = stage[k % 2][...].astype(BF16)


def _mixer_kernel(x_ref, mod_ref, n1g_ref, w_in_ref, wsh_ref, w_a_ref, lng_ref, lnb_ref,
                  wsgu_ref, bsgu_ref, w_b_ref, cw_ref, cb_ref, clg_ref, clb_ref, w_c_ref,
                  w_o_ref, out_ref,
                  h_buf, cx_buf, cx_halo, glu_buf, ya_buf, yb_buf, gu_buf, ln_buf,
                  ln_c_buf, gate_buf):
    t, d = h_buf.shape
    s = pl.program_id(1)

    @pl.when(s == 0)
    def _():
        cx_halo[...] = jnp.zeros((SHORT_HALO, d), F32)
        glu_buf[0:CFM_HALO, :] = jnp.zeros((CFM_HALO, d), F32)

    col_blocks = [slice(j, j + COL_BLOCK) for j in range(0, d, COL_BLOCK)]

    conv_jobs = [(r0, l0) for l0 in range(0, d, LANES) for r0 in range(0, t, CONV_ROWS)]
    jobs_per_slot = -(-len(conv_jobs) // (3 * (d // SUB_BLOCK // SUBS_PER_DOT)))

    def run_conv_jobs(n):
        for _ in range(min(n, len(conv_jobs))):
            r0, l0 = conv_jobs.pop(0)
            ls = slice(l0, l0 + LANES)
            ln_c_buf[r0:r0 + CONV_ROWS, ls] = _causal_conv_rows(
                glu_buf[r0:r0 + CFM_HALO + CONV_ROWS, ls], cw_ref, ls) + cb_ref[:, ls]

    def h_dots(group):
        lo, width = W_IN_GROUPS[group]
        for j0 in range(0, d // SUB_BLOCK, SUBS_PER_DOT):
            cols = [(lo + i) * d + (j0 + jj) * SUB_BLOCK
                    for jj in range(SUBS_PER_DOT) for i in range(width)]
            rhs = jnp.concatenate([w_in_ref[:, c:c + SUB_BLOCK] for c in cols], axis=1)
            yield j0, (_dot(h_buf[...], rhs), width)

    def part(res_width, i, jj):
        res, width = res_width
        c0 = (jj * width + i) * SUB_BLOCK
        return res[:, c0:c0 + SUB_BLOCK]

    def sub_blocks(j0):
        for jj in range(SUBS_PER_DOT):
            yield jj, slice((j0 + jj) * SUB_BLOCK, (j0 + jj + 1) * SUB_BLOCK)

    h_buf[...] = _modulated_rms(x_ref[0], n1g_ref[...], mod_ref[0, 0:1, :],
                                mod_ref[0, 1:2, :]).astype(BF16)

    for j0, res in h_dots("glu"):
        for jj, cs in sub_blocks(j0):
            glu_buf[CFM_HALO:CFM_HALO + t, cs] = part(res, 0, jj) * _sigmoid(part(res, 1, jj))

    for j0, res in h_dots("gates"):
        run_conv_jobs(jobs_per_slot)
        for jj, cs in sub_blocks(j0):
            for i in range(N_BRANCH):
                gate_buf[i, :, cs] = _sigmoid(part(res, i, jj)).astype(BF16)

    for j0, res in h_dots("short"):
        run_conv_jobs(jobs_per_slot)
        for jj, cs in sub_blocks(j0):
            cx = part(res, 1, jj) * part(res, 2, jj)
            cx_buf[0:SHORT_HALO, :] = cx_halo[:, cs]
            cx_buf[SHORT_HALO:SHORT_HALO + t, :] = cx
            cx_halo[:, cs] = cx_buf[t:t + SHORT_HALO, :]
            win = cx_buf[...]
            conv = wsh_ref[SHORT_K - 1:SHORT_K, cs] * cx
            for lag in range(1, SHORT_K):
                k = SHORT_K - 1 - lag
                conv = conv + wsh_ref[k:k + 1, cs] * pltpu.roll(win, lag, axis=0)[SHORT_HALO:, :]
            ya_buf[:, cs] = (part(res, 0, jj) * conv).astype(BF16)

    for j0, res in h_dots("sgu"):
        run_conv_jobs(jobs_per_slot)
        for jj, cs in sub_blocks(j0):
            gu_buf[:, cs] = _gelu_tanh(part(res, 0, jj)).astype(BF16)
            ln_buf[:, cs] = _gelu_tanh(part(res, 1, jj))
    assert not conv_jobs

    mu, rstd = _row_stats(ln_buf[...], d)
    tri = (lax.broadcasted_iota(jnp.int32, (CHUNK, CHUNK), 0)
           >= lax.broadcasted_iota(jnp.int32, (CHUNK, CHUNK), 1))
    gd = d // N_SGU_GROUPS
    chunks = [slice(n * CHUNK, (n + 1) * CHUNK) for n in range(t // CHUNK)]
    for g in range(N_SGU_GROUPS):
        gs = slice(g * gd, (g + 1) * gd)
        vn = ((ln_buf[:, gs] - mu) * rstd * lng_ref[:, gs] + lnb_ref[:, gs]).astype(BF16)
        wg = jnp.where(tri, wsgu_ref[g].astype(F32), 0.0).astype(BF16)
        mixed = _dot(wg, jnp.concatenate([vn[rs, :] for rs in chunks], axis=1))
        bg = bsgu_ref[:, g:g + 1]
        for n, rs in enumerate(chunks):
            yb_buf[rs, gs] = (gu_buf[rs, gs].astype(F32)
                              * (mixed[:, n * gd:(n + 1) * gd] + bg)).astype(BF16)

    for cs in col_blocks:
        m = gate_buf[0, :, cs].astype(F32) * _dot(ya_buf[...], w_a_ref[:, cs])
        m = m + gate_buf[1, :, cs].astype(F32) * _dot(yb_buf[...], w_b_ref[:, cs])
        ln_buf[:, cs] = m

    mu, rstd = _row_stats(ln_c_buf[...], d)
    yc_buf = ya_buf
    yc_buf[...] = _silu((ln_c_buf[...] - mu) * rstd * clg_ref[...] + clb_ref[...]).astype(BF16)
    mg_buf = gu_buf
    for cs in col_blocks:
        m = ln_buf[:, cs] + gate_buf[2, :, cs].astype(F32) * _dot(yc_buf[...], w_c_ref[:, cs])
        mg_buf[:, cs] = m.astype(BF16)
    for cs in col_blocks:
        o = _dot(mg_buf[...], w_o_ref[:, cs])
        out_ref[0, :, cs] = x_ref[0, :, cs] + mod_ref[0, 2:3, cs] * o

    glu_buf[0:CFM_HALO, :] = glu_buf[t:t + CFM_HALO, :]


def _cast_kernel(*refs):
    n = len(refs) // 2
    for src, dst in zip(refs[:n], refs[n:]):
        dst[...] = src[0].astype(BF16)


def _layer_to_bf16(ws, layer, row_block):
    _, rows, cols = ws[0].shape
    col_block = min(cols, CAST_COL_BLOCK)
    return pl.pallas_call(
        _cast_kernel,
        grid=(rows // row_block, cols // col_block),
        in_specs=[pl.BlockSpec((1, row_block, col_block), lambda i, j: (layer, i, j)) for _ in ws],
        out_specs=[pl.BlockSpec((row_block, col_block), lambda i, j: (i, j)) for _ in ws],
        out_shape=[jax.ShapeDtypeStruct((rows, cols), BF16) for _ in ws],
        name="weights_to_bf16",
    )(*ws)


def _resident(shape):
    zeros = (0,) * len(shape)
    return pl.BlockSpec(shape, lambda b, s: zeros, pipeline_mode=pl.Buffered(1))


def _mixer(x, mod, n1g, w_in, wsh, w_a, lng, lnb, wsgu, bsgu_t, w_b, cw, cb, clg, clb, w_c, w_o):
    bn, sq, d = x.shape
    t = TOK_TILE
    params = (n1g, w_in, wsh, w_a, lng, lnb, wsgu, bsgu_t, w_b, cw, cb, clg, clb, w_c, w_o)
    return pl.pallas_call(
        _mixer_kernel,
        grid=(bn, sq // t),
        in_specs=[pl.BlockSpec((1, t, d), lambda b, s: (b, s, 0)),
                  pl.BlockSpec((1, N_MOD, d), lambda b, s: (b, 0, 0))]
                 + [_resident(p.shape) for p in params],
        out_specs=pl.BlockSpec((1, t, d), lambda b, s: (b, s, 0)),
        out_shape=jax.ShapeDtypeStruct(x.shape, F32),
        scratch_shapes=[
            pltpu.VMEM((t, d), BF16),
            pltpu.VMEM((SHORT_HALO + t, SUB_BLOCK), F32),
            pltpu.VMEM((SHORT_HALO, d), F32),
            pltpu.VMEM((CFM_HALO + t, d), F32),
            pltpu.VMEM((t, d), BF16),
            pltpu.VMEM((t, d), BF16),
            pltpu.VMEM((t, d), BF16),
            pltpu.VMEM((t, d), F32),
            pltpu.VMEM((t, d), F32),
            pltpu.VMEM((N_BRANCH, t, d), BF16),
        ],
        compiler_params=pltpu.CompilerParams(
            dimension_semantics=("arbitrary", "arbitrary"),
            vmem_limit_bytes=VMEM_LIMIT_BYTES),
        name="token_mixer",
    )(x, mod, *params)


def _ffn_kernel(x_ref, mod_ref, n2g_ref, w_in_hbm, w_out_hbm, fg_ref, out_ref, h_buf, a_buf,
                w_in_ref, w_out_ref, stage_in, stage_out, w_sem, *, layer, final_norm):
    t, d = h_buf.shape
    d_ff = a_buf.shape[1]

    @pl.when(_is_first_step())
    def _():
        for hbm, vmem, stage in ((w_in_hbm, w_in_ref, stage_in), (w_out_hbm, w_out_ref, stage_out)):
            rows = stage.shape[1]
            jobs = [(hbm.at[layer, pl.ds(r0, rows), :], vmem.at[pl.ds(r0, rows), :])
                    for r0 in range(0, vmem.shape[0], rows)]
            _load_weights_bf16(jobs, (stage.at[0], stage.at[1]), w_sem)

    h_buf[...] = _modulated_rms(x_ref[0], n2g_ref[...], mod_ref[0, 3:4, :],
                                mod_ref[0, 4:5, :]).astype(BF16)
    for j0 in range(0, d_ff, FFN_SUBS_PER_DOT * SUB_BLOCK):
        subs = range(j0, min(j0 + FFN_SUBS_PER_DOT * SUB_BLOCK, d_ff), SUB_BLOCK)
        rhs = jnp.concatenate([w_in_ref[:, c:c + SUB_BLOCK] for j in subs for c in (j, d_ff + j)],
                              axis=1)
        res = _dot(h_buf[...], rhs)
        for k, j in enumerate(subs):
            gate = res[:, 2 * k * SUB_BLOCK:(2 * k + 1) * SUB_BLOCK]
            up = res[:, (2 * k + 1) * SUB_BLOCK:(2 * k + 2) * SUB_BLOCK]
            a_buf[:, j:j + SUB_BLOCK] = (_silu(gate) * up).astype(BF16)
    for j in range(0, d, COL_BLOCK):
        cs = slice(j, j + COL_BLOCK)
        o = _dot(a_buf[...], w_out_ref[:, cs])
        out_ref[0, :, cs] = x_ref[0, :, cs] + mod_ref[0, 5:6, cs] * o
    if final_norm:
        y = out_ref[0]
        ms = jnp.mean(y * y, axis=-1, keepdims=True)
        out_ref[0] = y * lax.rsqrt(ms + EPS) * fg_ref[...]


def _ffn(x, mod, n2g, w_in, w_out, fg, *, layer, final_norm):
    bn, sq, d = x.shape
    d_ff = w_out.shape[1]
    t = FFN_TOK_TILE
    assert w_in.shape[1] % FFN_LOAD_CHUNKS == 0 and w_out.shape[1] % (16 * FFN_LOAD_CHUNKS) == 0
    hbm = pl.BlockSpec(memory_space=pl.ANY)
    return pl.pallas_call(
        functools.partial(_ffn_kernel, layer=layer, final_norm=final_norm),
        grid=(bn, sq // t),
        in_specs=[pl.BlockSpec((1, t, d), lambda b, s: (b, s, 0)),
                  pl.BlockSpec((1, N_MOD, d), lambda b, s: (b, 0, 0)),
                  _resident(n2g.shape), hbm, hbm, _resident(fg.shape)],
        out_specs=pl.BlockSpec((1, t, d), lambda b, s: (b, s, 0)),
        out_shape=jax.ShapeDtypeStruct(x.shape, F32),
        scratch_shapes=[
            pltpu.VMEM((t, d), BF16),
            pltpu.VMEM((t, d_ff), BF16),
            pltpu.VMEM(w_in.shape[1:], BF16),
            pltpu.VMEM(w_out.shape[1:], BF16),
            pltpu.VMEM((2, w_in.shape[1] // FFN_LOAD_CHUNKS, w_in.shape[2]), F32),
            pltpu.VMEM((2, w_out.shape[1] // FFN_LOAD_CHUNKS, w_out.shape[2]), F32),
            pltpu.SemaphoreType.DMA((2,)),
        ],
        compiler_params=pltpu.CompilerParams(
            dimension_semantics=("arbitrary", "arbitrary"),
            vmem_limit_bytes=VMEM_LIMIT_BYTES),
        name="swiglu_final" if final_norm else "swiglu",
    )(x, mod, n2g, w_in, w_out, fg)


def kernel(x, c, w_ada, b_ada, norm1_g, w_in, w_short, w_a_out, sgu_ln_g, sgu_ln_b, w_sgu, b_sgu,
           w_b_out, cfm_conv_w, cfm_conv_b, cfm_ln_g, cfm_ln_b, w_c_out, w_o, norm2_g, w_ffn_in,
           w_ffn_out, final_g):
    depth = w_in.shape[0]
    d = x.shape[-1]
    assert x.shape[1] % TOK_TILE == 0 and TOK_TILE % CHUNK == 0 and d % COL_BLOCK == 0
    assert x.shape[1] % FFN_TOK_TILE == 0
    assert TOK_TILE % CONV_ROWS == 0 and w_sgu.shape[-1] == CHUNK
    assert w_short.shape[1] == SHORT_K and cfm_conv_w.shape[1] == CFM_K

    mod = _modulation(c, w_ada, b_ada)
    row = lambda p: p.reshape(1, -1)
    fg = row(final_g)
    for l in range(depth):
        w_in_l, = _layer_to_bf16((w_in,), l, CAST_ROW_BLOCK)
        w_a_l, w_b_l, w_c_l, w_o_l = _layer_to_bf16((w_a_out, w_b_out, w_c_out, w_o), l,
                                                    2 * CAST_ROW_BLOCK)
        x = _mixer(x, mod[l], row(norm1_g[l]), w_in_l, w_short[l], w_a_l, row(sgu_ln_g[l]),
                   row(sgu_ln_b[l]), w_sgu[l].astype(BF16), b_sgu[l].T, w_b_l, cfm_conv_w[l],
                   row(cfm_conv_b[l]), row(cfm_ln_g[l]), row(cfm_ln_b[l]), w_c_l, w_o_l)
        x = _ffn(x, mod[l], row(norm2_g[l]), w_ffn_in, w_ffn_out, fg, layer=l,
                 final_norm=(l == depth - 1))
    return x
```
